```python
import math
import jax, jax.numpy as jnp
from jax import lax
import numpy as np

D_MODEL = 2048
BATCH = 4
SEQ = 2048
DEPTH = 4

N_EVEN = (DEPTH + 1) // 2
N_ODD = DEPTH // 2
EPS = 1e-6

POOL_WINDOWS = (2, 4, 8, 16)
POOL_GROUP = 128
POOL_WIDTH = len(POOL_WINDOWS) * POOL_GROUP
RET_HEADS = 6
RET_QK_DIM = 128
RET_V_DIM = 256
RET_CHUNK = 128
IN_EVEN = POOL_WIDTH + 2 * RET_HEADS * RET_QK_DIM + 2 * RET_HEADS * RET_V_DIM
MIX_EVEN = POOL_WIDTH + RET_HEADS * RET_V_DIM
SGU_CHUNK = 128
SGU_GROUPS = 8
SGU_GROUP_DIM = 128
SGU_WIDTH = SGU_GROUPS * SGU_GROUP_DIM
DIFF_HEADS = 4
DIFF_HEAD_DIM = 128
DIFF_V_DIM = 2 * DIFF_HEAD_DIM
Q_BLOCK = 128
N_BUCKETS = 32
MAX_DISTANCE = 128
IN_ODD = 2 * SGU_WIDTH + 2 * DIFF_HEADS * 2 * DIFF_HEAD_DIM + DIFF_HEADS * DIFF_V_DIM
MIX_ODD = SGU_WIDTH + DIFF_HEADS * DIFF_V_DIM
D_FF = 5632
CONV_WIDTH = 3

kernel_name = 'hybrid_pool_retention_sgu_diffattn'


def rms_norm(x, g):
    xf = x.astype(jnp.float32)
    y = xf * lax.rsqrt(jnp.mean(xf * xf, axis=-1, keepdims=True) + EPS)
    return (y * g.astype(jnp.float32)).astype(x.dtype)


def layer_norm(x, g):
    xf = x.astype(jnp.float32)
    mu = jnp.mean(xf, axis=-1, keepdims=True)
    var = jnp.mean(jnp.square(xf - mu), axis=-1, keepdims=True)
    return ((xf - mu) * lax.rsqrt(var + EPS) * g.astype(jnp.float32)).astype(x.dtype)


def rotary(x):
    s, d = x.shape[1], x.shape[-1]
    half = d // 2
    inv = 1.0 / (10000.0 ** (jnp.arange(half, dtype=jnp.float32) / half))
    ang = jnp.arange(s, dtype=jnp.float32)[:, None] * inv[None, :]
    cos = jnp.cos(ang)[None, :, None, :].astype(x.dtype)
    sin = jnp.sin(ang)[None, :, None, :].astype(x.dtype)
    x1, x2 = x[..., :half], x[..., half:]
    return jnp.concatenate([x1 * cos - x2 * sin, x1 * sin + x2 * cos], axis=-1)


def pool_mixer(u, pool_w, pool_scale):
    b, s, _ = u.shape
    ug = u.reshape(b, s, len(POOL_WINDOWS), POOL_GROUP)
    ugf = ug.astype(jnp.float32)
    cs = jnp.cumsum(ugf, axis=1)
    t = jnp.arange(1, s + 1, dtype=jnp.float32)
    pooled = []
    for gi, w in enumerate(POOL_WINDOWS):
        c = cs[:, :, gi]
        lag = jnp.pad(c, ((0, 0), (w, 0), (0, 0)))[:, :s]
        pooled.append((c - lag) / jnp.minimum(t, float(w))[None, :, None])
    pooled = jnp.stack(pooled, axis=2)
    y = (pooled - ugf).astype(u.dtype)
    y = jnp.einsum('bsgc,gcd->bsgd', y, pool_w)
    return y.reshape(b, s, POOL_WIDTH) * pool_scale


def retention(q, k, v):
    b, s, h, dk = q.shape
    dv = v.shape[-1]
    c = RET_CHUNK
    n = s // c
    dt = v.dtype
    log_g = jnp.log(1.0 - 2.0 ** (-5.0 - jnp.arange(h, dtype=jnp.float32)))
    idx = jnp.arange(c, dtype=jnp.float32)
    diff = idx[:, None] - idx[None, :]
    intra = jnp.where(diff >= 0, jnp.exp(log_g[:, None, None] * jnp.maximum(diff, 0.0)), 0.0)
    q_dec = jnp.exp(log_g[:, None] * (idx[None, :] + 1.0))
    k_dec = jnp.exp(log_g[:, None] * (c - 1.0 - idx[None, :]))
    chunk_dec = jnp.exp(log_g * c)
    qc = q.reshape(b, n, c, h, dk)
    kc = k.reshape(b, n, c, h, dk)
    vc = v.reshape(b, n, c, h, dv)
    scores = jnp.einsum('bnchd,bnmhd->bnhcm', qc, kc) * intra.astype(dt)
    inner = jnp.einsum('bnhcm,bnmhe->bnche', scores, vc)
    kv = jnp.einsum('bnmhd,hm,bnmhe->bnhde', kc, k_dec.astype(dt), vc)

    def step(state, kv_n):
        return state * chunk_dec[None, :, None, None] + kv_n, state

    _, prev = lax.scan(step, jnp.zeros((b, h, dk, dv), jnp.float32),
                       jnp.moveaxis(kv, 1, 0).astype(jnp.float32))
    prev = jnp.moveaxis(prev, 0, 1).astype(dt)
    cross = jnp.einsum('bnchd,hc,bnhde->bnche', qc, q_dec.astype(dt), prev)
    return (inner + cross).reshape(b, s, h, dv)


def pool_retention_mixer(hn, w_in, w_out, pool_w, pool_scale, ret_gn_g):
    b, s, _ = hn.shape
    z = hn @ w_in
    i0 = POOL_WIDTH
    i1 = i0 + RET_HEADS * RET_QK_DIM
    i2 = i1 + RET_HEADS * RET_QK_DIM
    i3 = i2 + RET_HEADS * RET_V_DIM
    a_out = pool_mixer(z[..., :i0], pool_w, pool_scale)
    q = rotary(z[..., i0:i1].reshape(b, s, RET_HEADS, RET_QK_DIM))
    k = rotary(z[..., i1:i2].reshape(b, s, RET_HEADS, RET_QK_DIM)) * (RET_QK_DIM ** -0.5)
    v = z[..., i2:i3].reshape(b, s, RET_HEADS, RET_V_DIM)
    g = z[..., i3:]
    r = retention(q, k, v)
    r = layer_norm(r, ret_gn_g.reshape(RET_HEADS, RET_V_DIM)).reshape(b, s, RET_HEADS * RET_V_DIM)
    b_out = r * jax.nn.silu(g)
    return jnp.concatenate([a_out, b_out], axis=-1) @ w_out


def spatial_gating(zu, zv, ln_g, w_s, b_s):
    b, s, _ = zu.shape
    n = s // SGU_CHUNK
    v = layer_norm(zv, ln_g).reshape(b, n, SGU_CHUNK, SGU_GROUPS, SGU_GROUP_DIM)
    causal = jnp.tril(jnp.ones((SGU_CHUNK, SGU_CHUNK), dtype=w_s.dtype))
    sv = jnp.einsum('gcm,bnmgd->bncgd', w_s * causal, v) + b_s.T[None, None, :, :, None]
    return zu * sv.reshape(b, s, SGU_WIDTH)


def t5_bucket(rel):
    n = jnp.maximum(rel, 0)
    max_exact = N_BUCKETS // 2
    large = max_exact + (jnp.log(jnp.maximum(n, 1).astype(jnp.float32) / max_exact)
                         / math.log(MAX_DISTANCE / max_exact) * (N_BUCKETS - max_exact)).astype(jnp.int32)
    large = jnp.minimum(large, N_BUCKETS - 1)
    return jnp.where(n < max_exact, n, large)


def diff_attention(q, k, v, lam, rel_bias):
    b, s, h, _, dh = q.shape
    nb = s // Q_BLOCK
    kpos = jnp.arange(s)
    qb = jnp.moveaxis((q * (dh ** -0.5)).reshape(b, nb, Q_BLOCK, h, 2, dh), 1, 0)

    def block(args):
        q_blk, i = args
        qpos = i * Q_BLOCK + jnp.arange(Q_BLOCK)
        rel = qpos[:, None] - kpos[None, :]
        bias = jnp.transpose(rel_bias[t5_bucket(rel)], (2, 0, 1))
        logits = jnp.einsum('bqhjd,bkhjd->bhjqk', q_blk, k).astype(jnp.float32)
        logits = logits + bias.astype(jnp.float32)[None, :, None]
        logits = jnp.where(rel >= 0, logits, -1e30)
        p = jax.nn.softmax(logits, axis=-1)
        attn = (p[:, :, 0] - lam * p[:, :, 1]).astype(v.dtype)
        return jnp.einsum('bhqk,bkhe->bqhe', attn, v)

    out = lax.map(block, (qb, jnp.arange(nb)))
    return jnp.moveaxis(out, 0, 1).reshape(b, s, h, v.shape[-1])


def sgu_diff_mixer(hn, w_in, w_out, sgu_ln_g, sgu_w, sgu_b, lq1, lk1, lq2, lk2, subln_g, rel_bias, layer_idx):
    b, s, _ = hn.shape
    z = hn @ w_in
    j0 = 2 * SGU_WIDTH
    j1 = j0 + DIFF_HEADS * 2 * DIFF_HEAD_DIM
    j2 = j1 + DIFF_HEADS * 2 * DIFF_HEAD_DIM
    zc = jax.nn.gelu(z[..., :j0])
    c_out = spatial_gating(zc[..., :SGU_WIDTH], zc[..., SGU_WIDTH:], sgu_ln_g, sgu_w, sgu_b)
    q = z[..., j0:j1].reshape(b, s, DIFF_HEADS, 2, DIFF_HEAD_DIM)
    k = z[..., j1:j2].reshape(b, s, DIFF_HEADS, 2, DIFF_HEAD_DIM)
    v = z[..., j2:].reshape(b, s, DIFF_HEADS, DIFF_V_DIM)
    lam_init = 0.8 - 0.6 * math.exp(-0.3 * layer_idx)
    lam = (jnp.exp(jnp.sum(lq1.astype(jnp.float32) * lk1.astype(jnp.float32)))
           - jnp.exp(jnp.sum(lq2.astype(jnp.float32) * lk2.astype(jnp.float32))) + lam_init)
    d = diff_attention(q, k, v, lam, rel_bias)
    d_out = (rms_norm(d, subln_g) * (1.0 - lam_init)).reshape(b, s, DIFF_HEADS * DIFF_V_DIM)
    return jnp.concatenate([c_out, d_out], axis=-1) @ w_out


def conv_ffn(hn, w_up, conv_w, conv_b, w_down):
    a = hn @ w_up
    ch = a.shape[-1]
    a = lax.conv_general_dilated(a, conv_w[:, None, :], window_strides=(1,),
                                 padding=[(CONV_WIDTH - 1, 0)],
                                 dimension_numbers=('NWC', 'WIO', 'NWC'),
                                 feature_group_count=ch) + conv_b
    gate, val = jnp.split(a, 2, axis=-1)
    return (jax.nn.silu(gate) * val) @ w_down


def setup_inputs(seed: int = 0) -> dict:
    key = jax.random.key(seed)
    ks = jax.random.split(key, 24)
    f32 = jnp.float32

    def nrm(k, shape, scale):
        return jax.random.normal(k, shape, f32) * scale

    return {
        'x': nrm(ks[0], (BATCH, SEQ, D_MODEL), 1.0),
        'w_in_even': nrm(ks[1], (N_EVEN, D_MODEL, IN_EVEN), D_MODEL ** -0.5),
        'w_out_even': nrm(ks[2], (N_EVEN, MIX_EVEN, D_MODEL), MIX_EVEN ** -0.5),
        'pool_w': nrm(ks[3], (N_EVEN, len(POOL_WINDOWS), POOL_GROUP, POOL_GROUP), POOL_GROUP ** -0.5),
        'pool_scale': 1.0 + nrm(ks[4], (N_EVEN, POOL_WIDTH), 0.02),
        'ret_gn_g': 1.0 + nrm(ks[5], (N_EVEN, RET_HEADS * RET_V_DIM), 0.02),
        'w_in_odd': nrm(ks[6], (N_ODD, D_MODEL, IN_ODD), D_MODEL ** -0.5),
        'w_out_odd': nrm(ks[7], (N_ODD, MIX_ODD, D_MODEL), MIX_ODD ** -0.5),
        'sgu_ln_g': 1.0 + nrm(ks[8], (N_ODD, SGU_WIDTH), 0.02),
        'sgu_w': nrm(ks[9], (N_ODD, SGU_GROUPS, SGU_CHUNK, SGU_CHUNK), SGU_CHUNK ** -0.5),
        'sgu_b': 1.0 + nrm(ks[10], (N_ODD, SGU_GROUPS, SGU_CHUNK), 0.1),
        'lam_q1': nrm(ks[11], (N_ODD, DIFF_HEAD_DIM), 0.1),
        'lam_k1': nrm(ks[12], (N_ODD, DIFF_HEAD_DIM), 0.1),
        'lam_q2': nrm(ks[13], (N_ODD, DIFF_HEAD_DIM), 0.1),
        'lam_k2': nrm(ks[14], (N_ODD, DIFF_HEAD_DIM), 0.1),
        'diff_subln_g': 1.0 + nrm(ks[15], (N_ODD, DIFF_V_DIM), 0.02),
        'rel_bias': nrm(ks[16], (N_BUCKETS, DIFF_HEADS), 0.5),
        'mix_norm_g': 1.0 + nrm(ks[17], (DEPTH, D_MODEL), 0.02),
        'ffn_norm_g': 1.0 + nrm(ks[18], (DEPTH, D_MODEL), 0.02),
        'w_up': nrm(ks[19], (DEPTH, D_MODEL, 2 * D_FF), D_MODEL ** -0.5),
        'conv_w': nrm(ks[20], (DEPTH, CONV_WIDTH, 2 * D_FF), CONV_WIDTH ** -0.5),
        'conv_b': nrm(ks[21], (DEPTH, 2 * D_FF), 0.02),
        'w_down': nrm(ks[22], (DEPTH, D_FF, D_MODEL), D_FF ** -0.5),
        'final_norm_g': 1.0 + nrm(ks[23], (D_MODEL,), 0.02),
    }


def reference(x, w_in_even, w_out_even, pool_w, pool_scale, ret_gn_g, w_in_odd, w_out_odd,
              sgu_ln_g, sgu_w, sgu_b, lam_q1, lam_k1, lam_q2, lam_k2, diff_subln_g, rel_bias,
              mix_norm_g, ffn_norm_g, w_up, conv_w, conv_b, w_down, final_norm_g):
    h = x
    for i in range(DEPTH):
        hn = rms_norm(h, mix_norm_g[i])
        if i % 2 == 0:
            e = i // 2
            h = h + pool_retention_mixer(hn, w_in_even[e], w_out_even[e], pool_w[e],
                                         pool_scale[e], ret_gn_g[e])
        else:
            o = i // 2
            h = h + sgu_diff_mixer(hn, w_in_odd[o], w_out_odd[o], sgu_ln_g[o], sgu_w[o], sgu_b[o],
                                   lam_q1[o], lam_k1[o], lam_q2[o], lam_k2[o], diff_subln_g[o],
                                   rel_bias, i)
        h = h + conv_ffn(rms_norm(h, ffn_norm_g[i]), w_up[i], conv_w[i], conv_b[i], w_down[i])
    return rms_norm(h, final_norm_g)
```

```python
import functools
import math

import jax
import jax.numpy as jnp
from jax import lax
from jax.experimental import pallas as pl
from jax.experimental.pallas import tpu as pltpu

EPS = 1e-6
POOL_WINDOWS = (2, 4, 8, 16)
POOL_GROUP = 128
POOL_WIDTH = len(POOL_WINDOWS) * POOL_GROUP
POOL_HALO = 16
RET_HEADS = 6
RET_QK_DIM = 128
RET_V_DIM = 256
RET_CHUNK = 128
SGU_CHUNK = 128
SGU_GROUPS = 8
SGU_GROUP_DIM = 128
SGU_WIDTH = SGU_GROUPS * SGU_GROUP_DIM
DIFF_HEADS = 4
DIFF_HEAD_DIM = 128
DIFF_V_DIM = 2 * DIFF_HEAD_DIM
N_BUCKETS = 32
MAX_DISTANCE = 128
CONV_WIDTH = 3
NEG_INF = -1e30

V7X_VMEM_BYTES = 64 * 1024 * 1024
VMEM_LIMIT = V7X_VMEM_BYTES - 8 * 1024 * 1024
SUBLANES = 8

F32 = jnp.float32
BF16 = jnp.bfloat16


def _params(*semantics):
    return pltpu.CompilerParams(dimension_semantics=semantics, vmem_limit_bytes=VMEM_LIMIT)


def _rms(x, g):
    return x * lax.rsqrt(jnp.mean(x * x, axis=-1, keepdims=True) + EPS) * g


def _layer_norm(x, g):
    mu = jnp.mean(x, axis=-1, keepdims=True)
    xc = x - mu
    var = jnp.mean(xc * xc, axis=-1, keepdims=True)
    return xc * lax.rsqrt(var + EPS) * g


def _silu(x):
    return x * (1.0 / (1.0 + jnp.exp(-x)))


def _gelu_tanh(x):
    c = math.sqrt(2.0 / math.pi)
    return 0.5 * x * (1.0 + jnp.tanh(c * (x + 0.044715 * (x * x * x))))


def _rmsnorm_kernel(x_ref, g_ref, o_ref):
    o_ref[...] = _rms(x_ref[...], g_ref[...]).astype(o_ref.dtype)


def rmsnorm(x, g, *, tm=512, out_dtype=BF16):
    m, d = x.shape
    return pl.pallas_call(
        _rmsnorm_kernel,
        grid=(m // tm,),
        in_specs=[pl.BlockSpec((tm, d), lambda i: (i, 0)),
                  pl.BlockSpec((1, d), lambda i: (0, 0))],
        out_specs=pl.BlockSpec((tm, d), lambda i: (i, 0)),
        out_shape=jax.ShapeDtypeStruct((m, d), out_dtype),
        compiler_params=_params("parallel"),
        name="rmsnorm",
    )(x, g.reshape(1, d))


def _matmul_kernel(x_ref, w_ref, o_ref):
    o_ref[...] = jnp.dot(x_ref[...], w_ref[...], preferred_element_type=F32).astype(o_ref.dtype)


def matmul(x, w, *, tm=1024, tn=1024, out_dtype=BF16):
    m, k = x.shape
    n = w.shape[1]
    return pl.pallas_call(
        _matmul_kernel,
        grid=(m // tm, n // tn),
        in_specs=[pl.BlockSpec((tm, k), lambda i, j: (i, 0)),
                  pl.BlockSpec((k, tn), lambda i, j: (0, j))],
        out_specs=pl.BlockSpec((tm, tn), lambda i, j: (i, j)),
        out_shape=jax.ShapeDtypeStruct((m, n), out_dtype),
        compiler_params=_params("parallel", "parallel"),
        name="in_proj",
    )(x, w)


def _proj_res_norm_kernel(*refs, n_pieces, emit_h):
    x_refs = refs[:n_pieces]
    w_ref, h_ref, g_ref = refs[n_pieces:n_pieces + 3]
    out_refs = refs[n_pieces + 3:]
    if n_pieces > 1:
        x = jnp.concatenate([r[...] for r in x_refs], axis=-1)
    else:
        x = x_refs[0][...]
    h_new = h_ref[...] + jnp.dot(x, w_ref[...], preferred_element_type=F32)
    if emit_h:
        out_refs[0][...] = h_new
    out_refs[-1][...] = _rms(h_new, g_ref[...]).astype(out_refs[-1].dtype)


def proj_res_norm(pieces, w, h, g, *, tm, norm_dtype=BF16, emit_h=True):
    m, d = h.shape
    k = w.shape[0]
    assert sum(p.shape[1] for p in pieces) == k
    in_specs = [pl.BlockSpec((tm, p.shape[1]), lambda i: (i, 0)) for p in pieces]
    in_specs += [pl.BlockSpec((k, d), lambda i: (0, 0), pipeline_mode=pl.Buffered(1)),
                 pl.BlockSpec((tm, d), lambda i: (i, 0)),
                 pl.BlockSpec((1, d), lambda i: (0, 0))]
    row_spec = pl.BlockSpec((tm, d), lambda i: (i, 0))
    norm_shape = jax.ShapeDtypeStruct((m, d), norm_dtype)
    if emit_h:
        out_specs, out_shape = [row_spec, row_spec], [jax.ShapeDtypeStruct((m, d), F32), norm_shape]
    else:
        out_specs, out_shape = row_spec, norm_shape
    return pl.pallas_call(
        functools.partial(_proj_res_norm_kernel, n_pieces=len(pieces), emit_h=emit_h),
        grid=(m // tm,),
        in_specs=in_specs,
        out_specs=out_specs,
        out_shape=out_shape,
        compiler_params=_params("parallel"),
        name="proj_res_norm",
    )(*pieces, w, h, g.reshape(1, d))


def _shift_rows(a, prev, shift):
    rolled = pltpu.roll(a, shift, 0)
    row = lax.broadcasted_iota(jnp.int32, (SUBLANES, a.shape[1]), 0)
    head = rolled[:SUBLANES]
    for r in range(shift):
        head = jnp.where(row == r, prev[SUBLANES - shift + r:SUBLANES - shift + r + 1], head)
    return jnp.concatenate([head, rolled[SUBLANES:]], axis=0)


def _up_conv_gate_kernel(x_ref, wg_ref, wv_ref, cwg_ref, cwv_ref, cbg_ref, cbv_ref, o_ref,
                         carry_g, carry_v, *, tiles_per_seq):
    @pl.when(pl.program_id(1) % tiles_per_seq == 0)
    def _():
        carry_g[...] = jnp.zeros_like(carry_g)
        carry_v[...] = jnp.zeros_like(carry_v)

    x = x_ref[...]

    def conv_branch(w_ref, cw_ref, cb_ref, carry):
        a = jnp.dot(x, w_ref[...], preferred_element_type=F32)
        prev = carry[...]
        carry[...] = a[a.shape[0] - SUBLANES:]
        cw = cw_ref[...]
        return (a * cw[2:3] + _shift_rows(a, prev, 1) * cw[1:2]
                + _shift_rows(a, prev, 2) * cw[0:1] + cb_ref[...])

    gate = conv_branch(wg_ref, cwg_ref, cbg_ref, carry_g)
    val = conv_branch(wv_ref, cwv_ref, cbv_ref, carry_v)
    o_ref[...] = (_silu(gate) * val).astype(o_ref.dtype)


def up_conv_gate(x, w_up, conv_w, conv_b, *, seq, tm=512, tn=512):
    m, k = x.shape
    d_ff = w_up.shape[1] // 2
    nj = d_ff // tn
    conv_b = conv_b.reshape(1, 2 * d_ff)
    return pl.pallas_call(
        functools.partial(_up_conv_gate_kernel, tiles_per_seq=seq // tm),
        grid=(nj, m // tm),
        in_specs=[pl.BlockSpec((tm, k), lambda j, i: (i, 0)),
                  pl.BlockSpec((k, tn), lambda j, i: (0, j)),
                  pl.BlockSpec((k, tn), lambda j, i: (0, nj + j)),
                  pl.BlockSpec((CONV_WIDTH, tn), lambda j, i: (0, j)),
                  pl.BlockSpec((CONV_WIDTH, tn), lambda j, i: (0, nj + j)),
                  pl.BlockSpec((1, tn), lambda j, i: (0, j)),
                  pl.BlockSpec((1, tn), lambda j, i: (0, nj + j))],
        out_specs=pl.BlockSpec((tm, tn), lambda j, i: (i, j)),
        out_shape=jax.ShapeDtypeStruct((m, d_ff), BF16),
        scratch_shapes=[pltpu.VMEM((SUBLANES, tn), F32), pltpu.VMEM((SUBLANES, tn), F32)],
        compiler_params=_params("parallel", "arbitrary"),
        name="up_conv_gate",
    )(x, w_up, w_up, conv_w, conv_w, conv_b, conv_b)


def _pool_retention_kernel(cd_ref, z_ref, cos_ref, sin_ref, intra_ref, qdec_ref, kdec_ref, poolw_ref,
                           pscale_ref, gn_ref, a_ref, b_ref, u_carry, state, *, tile):
    t = pl.program_id(1)

    @pl.when(t == 0)
    def _():
        u_carry[...] = jnp.zeros_like(u_carry)
        state[...] = jnp.zeros_like(state)

    u = z_ref[:, :POOL_WIDTH].astype(F32)
    ext = jnp.concatenate([u_carry[...], u], axis=0)
    u_carry[...] = u[tile - POOL_HALO:]
    pos = (t * tile + 1 + lax.broadcasted_iota(jnp.int32, (tile, POOL_GROUP), 0)).astype(F32)
    for gi, w in enumerate(POOL_WINDOWS):
        cols = slice(gi * POOL_GROUP, (gi + 1) * POOL_GROUP)
        s = ext[:, cols]
        span = 1
        while span < w:
            s = s + pltpu.roll(s, span, 0)
            span *= 2
        pooled = s[POOL_HALO:] / jnp.minimum(pos, float(w))
        y = (pooled - u[:, cols]).astype(BF16)
        a = jnp.dot(y, poolw_ref[gi], preferred_element_type=F32)
        a_ref[:, cols] = (a * pscale_ref[:, cols]).astype(a_ref.dtype)

    q0 = POOL_WIDTH
    k0 = q0 + RET_HEADS * RET_QK_DIM
    v0 = k0 + RET_HEADS * RET_QK_DIM
    g0 = v0 + RET_HEADS * RET_V_DIM
    half = RET_QK_DIM // 2

    def chunk_body(c, carry):
        r0 = pl.multiple_of(c * RET_CHUNK, RET_CHUNK)
        rows = pl.ds(r0, RET_CHUNK)
        cos = cos_ref[rows, :]
        sin = sin_ref[rows, :]
        for h in range(RET_HEADS):
            q = z_ref[rows, q0 + h * RET_QK_DIM:q0 + (h + 1) * RET_QK_DIM].astype(F32)
            k = z_ref[rows, k0 + h * RET_QK_DIM:k0 + (h + 1) * RET_QK_DIM].astype(F32)
            v = z_ref[rows, v0 + h * RET_V_DIM:v0 + (h + 1) * RET_V_DIM]
            gate = z_ref[rows, g0 + h * RET_V_DIM:g0 + (h + 1) * RET_V_DIM].astype(F32)
            qr = q * cos + pltpu.roll(q, half, 1) * sin
            kr = (k * cos + pltpu.roll(k, half, 1) * sin) * (RET_QK_DIM ** -0.5)
            scores = lax.dot_general(qr.astype(BF16), kr.astype(BF16), (((1,), (1,)), ((), ())),
                                     preferred_element_type=F32) * intra_ref[h]
            inner = jnp.dot(scores.astype(BF16), v, preferred_element_type=F32)
            st = state[h]
            cross = jnp.dot((qr * qdec_ref[h]).astype(BF16), st.astype(BF16), preferred_element_type=F32)
            kv = lax.dot_general((kr * kdec_ref[h]).astype(BF16), v, (((0,), (0,)), ((), ())),
                                 preferred_element_type=F32)
            state[h] = st * cd_ref[h] + kv
            vcols = slice(h * RET_V_DIM, (h + 1) * RET_V_DIM)
            r = _layer_norm(inner + cross, gn_ref[:, vcols])
            b_ref[rows, vcols] = (r * _silu(gate)).astype(b_ref.dtype)
        return carry

    lax.fori_loop(0, tile // RET_CHUNK, chunk_body, 0)


def _retention_tables(seq):
    half = RET_QK_DIM // 2
    inv = 1.0 / (10000.0 ** (jnp.arange(half, dtype=F32) / half))
    ang = jnp.arange(seq, dtype=F32)[:, None] * inv[None, :]
    cos, sin = jnp.cos(ang), jnp.sin(ang)
    cos_full = jnp.concatenate([cos, cos], axis=-1)
    sin_signed = jnp.concatenate([-sin, sin], axis=-1)
    c = RET_CHUNK
    log_g = jnp.log(1.0 - 2.0 ** (-5.0 - jnp.arange(RET_HEADS, dtype=F32)))
    idx = jnp.arange(c, dtype=F32)
    diff = idx[:, None] - idx[None, :]
    intra = jnp.where(diff >= 0, jnp.exp(log_g[:, None, None] * jnp.maximum(diff, 0.0)), 0.0)
    q_dec = jnp.exp(log_g[:, None] * (idx[None, :] + 1.0))
    k_dec = jnp.exp(log_g[:, None] * (c - 1.0 - idx[None, :]))
    chunk_dec = jnp.exp(log_g * c)
    q_dec = jnp.broadcast_to(q_dec[:, :, None], (RET_HEADS, c, RET_QK_DIM))
    k_dec = jnp.broadcast_to(k_dec[:, :, None], (RET_HEADS, c, RET_QK_DIM))
    return cos_full, sin_signed, intra, q_dec, k_dec, chunk_dec


def pool_retention(z, pool_w, pool_scale, ret_gn_g, *, batch, seq, tile=512):
    m, width = z.shape
    nt = seq // tile
    cos, sin, intra, q_dec, k_dec, chunk_dec = _retention_tables(seq)
    vw = RET_HEADS * RET_V_DIM
    const3 = lambda b, t: (0, 0, 0)
    return pl.pallas_call(
        functools.partial(_pool_retention_kernel, tile=tile),
        grid=(batch, nt),
        in_specs=[pl.BlockSpec(memory_space=pltpu.SMEM),
                  pl.BlockSpec((tile, width), lambda b, t: (b * nt + t, 0)),
                  pl.BlockSpec((tile, RET_QK_DIM), lambda b, t: (t, 0)),
                  pl.BlockSpec((tile, RET_QK_DIM), lambda b, t: (t, 0)),
                  pl.BlockSpec(intra.shape, const3),
                  pl.BlockSpec(q_dec.shape, const3),
                  pl.BlockSpec(k_dec.shape, const3),
                  pl.BlockSpec(pool_w.shape, const3),
                  pl.BlockSpec((1, POOL_WIDTH), lambda b, t: (0, 0)),
                  pl.BlockSpec((1, vw), lambda b, t: (0, 0))],
        out_specs=[pl.BlockSpec((tile, POOL_WIDTH), lambda b, t: (b * nt + t, 0)),
                   pl.BlockSpec((tile, vw), lambda b, t: (b * nt + t, 0))],
        out_shape=[jax.ShapeDtypeStruct((m, POOL_WIDTH), BF16), jax.ShapeDtypeStruct((m, vw), BF16)],
        scratch_shapes=[pltpu.VMEM((POOL_HALO, POOL_WIDTH), F32),
                        pltpu.VMEM((RET_HEADS, RET_QK_DIM, RET_V_DIM), F32)],
        compiler_params=_params("parallel", "arbitrary"),
        name="pool_retention",
    )(chunk_dec, z, cos, sin, intra, q_dec, k_dec, pool_w.astype(BF16),
      pool_scale.reshape(1, POOL_WIDTH), ret_gn_g.reshape(1, vw))


def _sgu_kernel(z_ref, lng_ref, ws_ref, bs_ref, o_ref, *, tile):
    zc = _gelu_tanh(z_ref[...].astype(F32))
    zu = zc[:, :SGU_WIDTH]
    v = _layer_norm(zc[:, SGU_WIDTH:], lng_ref[...]).astype(BF16)
    row = lax.broadcasted_iota(jnp.int32, (SGU_CHUNK, SGU_CHUNK), 0)
    col = lax.broadcasted_iota(jnp.int32, (SGU_CHUNK, SGU_CHUNK), 1)
    for g in range(SGU_GROUPS):
        wm = jnp.where(row >= col, ws_ref[g], 0.0).astype(BF16)
        bias = bs_ref[g]
        cols = slice(g * SGU_GROUP_DIM, (g + 1) * SGU_GROUP_DIM)
        for c in range(tile // SGU_CHUNK):
            rows = slice(c * SGU_CHUNK, (c + 1) * SGU_CHUNK)
            sv = jnp.dot(wm, v[rows, cols], preferred_element_type=F32) + bias
            o_ref[rows, cols] = (zu[rows, cols] * sv).astype(o_ref.dtype)


def spatial_gating(z, ln_g, w_s, b_s, *, tile=256):
    m = z.shape[0]
    return pl.pallas_call(
        functools.partial(_sgu_kernel, tile=tile),
        grid=(m // tile,),
        in_specs=[pl.BlockSpec((tile, 2 * SGU_WIDTH), lambda i: (i, 0)),
                  pl.BlockSpec((1, SGU_WIDTH), lambda i: (0, 0)),
                  pl.BlockSpec(w_s.shape, lambda i: (0, 0, 0)),
                  pl.BlockSpec((SGU_GROUPS, SGU_CHUNK, 1), lambda i: (0, 0, 0))],
        out_specs=pl.BlockSpec((tile, SGU_WIDTH), lambda i: (i, 0)),
        out_shape=jax.ShapeDtypeStruct((m, SGU_WIDTH), BF16),
        compiler_params=_params("parallel"),
        name="spatial_gating",
    )(z, ln_g.reshape(1, SGU_WIDTH), w_s, b_s.reshape(SGU_GROUPS, SGU_CHUNK, 1))


def _t5_bucket(rel):
    n = jnp.maximum(rel, 0)
    max_exact = N_BUCKETS // 2
    large = max_exact + (jnp.log(jnp.maximum(n, 1).astype(F32) / max_exact)
                         / math.log(MAX_DISTANCE / max_exact) * (N_BUCKETS - max_exact)).astype(jnp.int32)
    large = jnp.minimum(large, N_BUCKETS - 1)
    return jnp.where(n < max_exact, n, large)


def _bias_tiles_kernel(relb_ref, bucket_ref, o_ref):
    h = pl.program_id(0)
    bucket = bucket_ref[...]
    acc = jnp.zeros(bucket.shape, F32)
    for b in range(N_BUCKETS):
        acc = jnp.where(bucket == b, relb_ref[b, h], acc)
    o_ref[0] = acc


def rel_bias_tiles(rel_bias, *, tile):
    qk = jnp.arange(tile)[:, None] - jnp.arange(tile)[None, :]
    bucket = jnp.stack([_t5_bucket(qk), _t5_bucket(tile + qk)])
    return pl.pallas_call(
        _bias_tiles_kernel,
        grid=(DIFF_HEADS,),
        in_specs=[pl.BlockSpec(memory_space=pltpu.SMEM),
                  pl.BlockSpec((2, tile, tile), lambda h: (0, 0, 0))],
        out_specs=pl.BlockSpec((1, 2, tile, tile), lambda h: (h, 0, 0, 0)),
        out_shape=jax.ShapeDtypeStruct((DIFF_HEADS, 2, tile, tile), F32),
        compiler_params=_params("parallel"),
        name="rel_bias_tiles",
    )(rel_bias, bucket)


def _diff_attn_kernel(relb_ref, lq1_ref, lk1_ref, lq2_ref, lk2_ref, q_ref, k_ref, v_ref, bias_ref, subg_ref,
                      o_ref, m_ref, l_ref, acc_ref, *, tile, lam_init):
    h = pl.program_id(1)
    qi = pl.program_id(2)
    dh = DIFF_HEAD_DIM
    q = (q_ref[...].astype(F32) * (dh ** -0.5)).astype(BF16)
    far_bias = relb_ref[N_BUCKETS - 1, h]

    m_ref[...] = jnp.full(m_ref.shape, NEG_INF, F32)
    l_ref[...] = jnp.zeros_like(l_ref)
    acc_ref[...] = jnp.zeros_like(acc_ref)

    def kv_block(start, add_bias):
        kb = k_ref[pl.ds(start, tile), :]
        vb = v_ref[pl.ds(start, tile), :]
        for s in range(2):
            sc = lax.dot_general(q[:, s * dh:(s + 1) * dh], kb[:, s * dh:(s + 1) * dh],
                                 (((1,), (1,)), ((), ())), preferred_element_type=F32)
            sc = add_bias(sc)
            m_old = m_ref[s]
            m_new = jnp.maximum(m_old, jnp.max(sc, axis=-1, keepdims=True))
            alpha = jnp.exp(m_old - m_new)
            p = jnp.exp(sc - m_new)
            l_ref[s] = alpha * l_ref[s] + jnp.sum(p, axis=-1, keepdims=True)
            acc_ref[s] = alpha * acc_ref[s] + jnp.dot(p.astype(BF16), vb, preferred_element_type=F32)
            m_ref[s] = m_new

    def far_body(j, carry):
        kv_block(pl.multiple_of(j * tile, tile), lambda sc: sc + far_bias)
        return carry

    lax.fori_loop(0, jnp.maximum(qi - 1, 0), far_body, 0)

    @pl.when(qi >= 1)
    def _():
        kv_block(pl.multiple_of((qi - 1) * tile, tile), lambda sc: sc + bias_ref[0, 1])

    row = lax.broadcasted_iota(jnp.int32, (tile, tile), 0)
    col = lax.broadcasted_iota(jnp.int32, (tile, tile), 1)
    kv_block(pl.multiple_of(qi * tile, tile),
             lambda sc: jnp.where(row >= col, sc + bias_ref[0, 0], NEG_INF))

    lam = (jnp.exp(jnp.sum(lq1_ref[...] * lk1_ref[...], axis=-1, keepdims=True))
           - jnp.exp(jnp.sum(lq2_ref[...] * lk2_ref[...], axis=-1, keepdims=True)) + lam_init)
    d = acc_ref[0] / l_ref[0] - lam * (acc_ref[1] / l_ref[1])
    o_ref[...] = (_rms(d, subg_ref[...]) * (1.0 - lam_init)).astype(o_ref.dtype)


def diff_attention(z, bias_tiles, rel_bias, lq1, lk1, lq2, lk2, subln_g, *, batch, seq, layer_idx, tile):
    m = z.shape[0]
    nq = seq // tile
    w = 2 * DIFF_HEAD_DIM
    q_blk = 2 * SGU_WIDTH // w
    k_blk = q_blk + DIFF_HEADS
    v_blk = k_blk + DIFF_HEADS
    lam_init = 0.8 - 0.6 * math.exp(-0.3 * layer_idx)
    vec = lambda a: a.reshape(1, DIFF_HEAD_DIM)
    vec_spec = pl.BlockSpec((1, DIFF_HEAD_DIM), lambda b, h, i: (0, 0))
    return pl.pallas_call(
        functools.partial(_diff_attn_kernel, tile=tile, lam_init=lam_init),
        grid=(batch, DIFF_HEADS, nq),
        in_specs=[pl.BlockSpec(memory_space=pltpu.SMEM),
                  vec_spec, vec_spec, vec_spec, vec_spec,
                  pl.BlockSpec((tile, w), lambda b, h, i: (b * nq + i, q_blk + h)),
                  pl.BlockSpec((seq, w), lambda b, h, i: (b, k_blk + h)),
                  pl.BlockSpec((seq, w), lambda b, h, i: (b, v_blk + h)),
                  pl.BlockSpec((1, 2, tile, tile), lambda b, h, i: (h, 0, 0, 0)),
                  pl.BlockSpec((1, DIFF_V_DIM), lambda b, h, i: (0, 0))],
        out_specs=pl.BlockSpec((tile, DIFF_V_DIM), lambda b, h, i: (b * nq + i, h)),
        out_shape=jax.ShapeDtypeStruct((m, DIFF_HEADS * DIFF_V_DIM), BF16),
        scratch_shapes=[pltpu.VMEM((2, tile, 1), F32), pltpu.VMEM((2, tile, 1), F32),
                        pltpu.VMEM((2, tile, DIFF_V_DIM), F32)],
        compiler_params=_params("parallel", "parallel", "parallel"),
        name="diff_attention",
    )(rel_bias, vec(lq1), vec(lk1), vec(lq2), vec(lk2), z, z, z, bias_tiles, subln_g.reshape(1, DIFF_V_DIM))


ATTN_TILE = 256
ROW_TILE_OUT = 512
ROW_TILE_DOWN = 256


def kernel(x, w_in_even, w_out_even, pool_w, pool_scale, ret_gn_g, w_in_odd, w_out_odd, sgu_ln_g, sgu_w, sgu_b,
           lam_q1, lam_k1, lam_q2, lam_k2, diff_subln_g, rel_bias, mix_norm_g, ffn_norm_g, w_up, conv_w, conv_b,
           w_down, final_norm_g):
    batch, seq, d = x.shape
    depth = mix_norm_g.shape[0]
    h = x.reshape(batch * seq, d)
    hn = rmsnorm(h, mix_norm_g[0])
    bias_tiles = rel_bias_tiles(rel_bias, tile=ATTN_TILE)
    out = None
    for i in range(depth):
        if i % 2 == 0:
            e = i // 2
            z = matmul(hn, w_in_even[e].astype(BF16))
            pieces = pool_retention(z, pool_w[e], pool_scale[e], ret_gn_g[e], batch=batch, seq=seq)
            w_out = w_out_even[e]
        else:
            o = i // 2
            z = matmul(hn, w_in_odd[o].astype(BF16))
            c_out = spatial_gating(z, sgu_ln_g[o], sgu_w[o], sgu_b[o])
            d_out = diff_attention(z, bias_tiles, rel_bias, lam_q1[o], lam_k1[o], lam_q2[o], lam_k2[o],
                                   diff_subln_g[o], batch=batch, seq=seq, layer_idx=i, tile=ATTN_TILE)
            pieces = [c_out, d_out]
            w_out = w_out_odd[o]
        h, hn = proj_res_norm(pieces, w_out.astype(BF16), h, ffn_norm_g[i], tm=ROW_TILE_OUT)
        act = up_conv_gate(hn, w_up[i].astype(BF16), conv_w[i], conv_b[i], seq=seq)
        if i + 1 < depth:
            h, hn = proj_res_norm([act], w_down[i].astype(BF16), h, mix_norm_g[i + 1], tm=ROW_TILE_DOWN)
        else:
            out = proj_res_norm([act], w_down[i].astype(BF16), h, final_norm_g, tm=ROW_TILE_DOWN,
                                norm_dtype=x.dtype, emit_h=False)
    return out.reshape(batch, seq, d)
```

```python
import functools
import math

import jax
import jax.numpy as jnp
from jax import lax
from jax.experimental import pallas as pl
from jax.experimental.pallas import tpu as pltpu

EPS = 1e-6
POOL_WINDOWS = (2, 4, 8, 16)
POOL_GROUP = 128
POOL_WIDTH = len(POOL_WINDOWS) * POOL_GROUP
POOL_HALO = 16
RET_HEADS = 6
RET_QK_DIM = 128
RET_V_DIM = 256
RET_CHUNK = 128
SGU_CHUNK = 128
SGU_GROUPS = 8
SGU_GROUP_DIM = 128
SGU_WIDTH = SGU_GROUPS * SGU_GROUP_DIM
DIFF_HEADS = 4
DIFF_HEAD_DIM = 128
DIFF_V_DIM = 2 * DIFF_HEAD_DIM
N_BUCKETS = 32
MAX_DISTANCE = 128
CONV_WIDTH = 3
NEG_INF = -1e30

V7X_VMEM_BYTES = 64 * 1024 * 1024
VMEM_LIMIT = V7X_VMEM_BYTES - 8 * 1024 * 1024
SUBLANES = 8

F32 = jnp.float32
BF16 = jnp.bfloat16


def _params(*semantics):
    return pltpu.CompilerParams(dimension_semantics=semantics, vmem_limit_bytes=VMEM_LIMIT)


def _rms(x, g):
    return x * lax.rsqrt(jnp.mean(x * x, axis=-1, keepdims=True) + EPS) * g


def _layer_norm(x, g):
    mu = jnp.mean(x, axis=-1, keepdims=True)
    xc = x - mu
    var = jnp.mean(xc * xc, axis=-1, keepdims=True)
    return xc * lax.rsqrt(var + EPS) * g


def _silu(x):
    return x * (1.0 / (1.0 + jnp.exp(-x)))


def _gelu_tanh(x):
    c = math.sqrt(2.0 / math.pi)
    return 0.5 * x * (1.0 + jnp.tanh(c * (x + 0.044715 * (x * x * x))))


def _rmsnorm_kernel(x_ref, g_ref, o_ref):
    o_ref[...] = _rms(x_ref[...], g_ref[...]).astype(o_ref.dtype)


def rmsnorm(x, g, *, tm=512, out_dtype=BF16):
    m, d = x.shape
    return pl.pallas_call(
        _rmsnorm_kernel,
        grid=(m // tm,),
        in_specs=[pl.BlockSpec((tm, d), lambda i: (i, 0)),
                  pl.BlockSpec((1, d), lambda i: (0, 0))],
        out_specs=pl.BlockSpec((tm, d), lambda i: (i, 0)),
        out_shape=jax.ShapeDtypeStruct((m, d), out_dtype),
        compiler_params=_params("parallel"),
        name="rmsnorm",
    )(x, g.reshape(1, d))


def _matmul_kernel(x_ref, w_ref, o_ref):
    o_ref[...] = jnp.dot(x_ref[...], w_ref[...], preferred_element_type=F32).astype(o_ref.dtype)


def matmul(x, w, *, tm=1024, tn=1024, out_dtype=BF16):
    m, k = x.shape
    n = w.shape[1]
    return pl.pallas_call(
        _matmul_kernel,
        grid=(m // tm, n // tn),
        in_specs=[pl.BlockSpec((tm, k), lambda i, j: (i, 0)),
                  pl.BlockSpec((k, tn), lambda i, j: (0, j))],
        out_specs=pl.BlockSpec((tm, tn), lambda i, j: (i, j)),
        out_shape=jax.ShapeDtypeStruct((m, n), out_dtype),
        compiler_params=_params("parallel", "parallel"),
        name="in_proj",
    )(x, w)


def _proj_res_norm_kernel(*refs, n_pieces, emit_h):
    x_refs = refs[:n_pieces]
    w_ref, h_ref, g_ref = refs[n_pieces:n_pieces + 3]
    out_refs = refs[n_pieces + 3:]
    if n_pieces > 1:
        x = jnp.concatenate([r[...] for r in x_refs], axis=-1)
    else:
        x = x_refs[0][...]
    h_new = h_ref[...] + jnp.dot(x, w_ref[...], preferred_element_type=F32)
    if emit_h:
        out_refs[0][...] = h_new
    out_refs[-1][...] = _rms(h_new, g_ref[...]).astype(out_refs[-1].dtype)


def proj_res_norm(pieces, w, h, g, *, tm, norm_dtype=BF16, emit_h=True):
    m, d = h.shape
    k = w.shape[0]
    assert sum(p.shape[1] for p in pieces) == k
    in_specs = [pl.BlockSpec((tm, p.shape[1]), lambda i: (i, 0)) for p in pieces]
    in_specs += [pl.BlockSpec((k, d), lambda i: (0, 0), pipeline_mode=pl.Buffered(1)),
                 pl.BlockSpec((tm, d), lambda i: (i, 0)),
                 pl.BlockSpec((1, d), lambda i: (0, 0))]
    row_spec = pl.BlockSpec((tm, d), lambda i: (i, 0))
    norm_shape = jax.ShapeDtypeStruct((m, d), norm_dtype)
    if emit_h:
        out_specs, out_shape = [row_spec, row_spec], [jax.ShapeDtypeStruct((m, d), F32), norm_shape]
    else:
        out_specs, out_shape = row_spec, norm_shape
    return pl.pallas_call(
        functools.partial(_proj_res_norm_kernel, n_pieces=len(pieces), emit_h=emit_h),
        grid=(m // tm,),
        in_specs=in_specs,
        out_specs=out_specs,
        out_shape=out_shape,
        compiler_params=_params("parallel"),
        name="proj_res_norm",
    )(*pieces, w, h, g.reshape(1, d))


def _shift_rows(a, prev, shift):
    rolled = pltpu.roll(a, shift, 0)
    row = lax.broadcasted_iota(jnp.int32, (SUBLANES, a.shape[1]), 0)
    head = rolled[:SUBLANES]
    for r in range(shift):
        head = jnp.where(row == r, prev[SUBLANES - shift + r:SUBLANES - shift + r + 1], head)
    return jnp.concatenate([head, rolled[SUBLANES:]], axis=0)


def _up_conv_gate_kernel(x_ref, wg_ref, wv_ref, cwg_ref, cwv_ref, cbg_ref, cbv_ref, o_ref,
                         carry_g, carry_v, *, tiles_per_seq):
    @pl.when(pl.program_id(1) % tiles_per_seq == 0)
    def _():
        carry_g[...] = jnp.zeros_like(carry_g)
        carry_v[...] = jnp.zeros_like(carry_v)

    x = x_ref[...]

    def conv_branch(w_ref, cw_ref, cb_ref, carry):
        a = jnp.dot(x, w_ref[...], preferred_element_type=F32)
        prev = carry[...]
        carry[...] = a[a.shape[0] - SUBLANES:]
        cw = cw_ref[...]
        return (a * cw[2:3] + _shift_rows(a, prev, 1) * cw[1:2]
                + _shift_rows(a, prev, 2) * cw[0:1] + cb_ref[...])

    gate = conv_branch(wg_ref, cwg_ref, cbg_ref, carry_g)
    val = conv_branch(wv_ref, cwv_ref, cbv_ref, carry_v)
    o_ref[...] = (_silu(gate) * val).astype(o_ref.dtype)


def up_conv_gate(x, w_up, conv_w, conv_b, *, seq, tm=512, tn=512):
    m, k = x.shape
    d_ff = w_up.shape[1] // 2
    nj = d_ff // tn
    conv_b = conv_b.reshape(1, 2 * d_ff)
    return pl.pallas_call(
        functools.partial(_up_conv_gate_kernel, tiles_per_seq=seq // tm),
        grid=(nj, m // tm),
        in_specs=[pl.BlockSpec((tm, k), lambda j, i: (i, 0)),
                  pl.BlockSpec((k, tn), lambda j, i: (0, j)),
                  pl.BlockSpec((k, tn), lambda j, i: (0, nj + j)),
                  pl.BlockSpec((CONV_WIDTH, tn), lambda j, i: (0, j)),
                  pl.BlockSpec((CONV_WIDTH, tn), lambda j, i: (0, nj + j)),
                  pl.BlockSpec((1, tn), lambda j, i: (0, j)),
                  pl.BlockSpec((1, tn), lambda j, i: (0, nj + j))],
        out_specs=pl.BlockSpec((tm, tn), lambda j, i: (i, j)),
        out_shape=jax.ShapeDtypeStruct((m, d_ff), BF16),
        scratch_shapes=[pltpu.VMEM((SUBLANES, tn), F32), pltpu.VMEM((SUBLANES, tn), F32)],
        compiler_params=_params("parallel", "arbitrary"),
        name="up_conv_gate",
    )(x, w_up, w_up, conv_w, conv_w, conv_b, conv_b)


def _pool_retention_kernel(cd_ref, z_ref, cos_ref, sin_ref, intra_ref, qdec_ref, kdec_ref, poolw_ref,
                           pscale_ref, gn_ref, a_ref, b_ref, u_carry, state, *, tile):
    t = pl.program_id(1)

    @pl.when(t == 0)
    def _():
        u_carry[...] = jnp.zeros_like(u_carry)
        state[...] = jnp.zeros_like(state)

    u = z_ref[:, :POOL_WIDTH].astype(F32)
    ext = jnp.concatenate([u_carry[...], u], axis=0)
    u_carry[...] = u[tile - POOL_HALO:]
    pos = (t * tile + 1 + lax.broadcasted_iota(jnp.int32, (tile, POOL_GROUP), 0)).astype(F32)
    for gi, w in enumerate(POOL_WINDOWS):
        cols = slice(gi * POOL_GROUP, (gi + 1) * POOL_GROUP)
        s = ext[:, cols]
        span = 1
        while span < w:
            s = s + pltpu.roll(s, span, 0)
            span *= 2
        pooled = s[POOL_HALO:] / jnp.minimum(pos, float(w))
        y = (pooled - u[:, cols]).astype(BF16)
        a = jnp.dot(y, poolw_ref[gi], preferred_element_type=F32)
        a_ref[:, cols] = (a * pscale_ref[:, cols]).astype(a_ref.dtype)

    q0 = POOL_WIDTH
    k0 = q0 + RET_HEADS * RET_QK_DIM
    v0 = k0 + RET_HEADS * RET_QK_DIM
    g0 = v0 + RET_HEADS * RET_V_DIM
    half = RET_QK_DIM // 2

    def chunk_body(c, carry):
        r0 = pl.multiple_of(c * RET_CHUNK, RET_CHUNK)
        rows = pl.ds(r0, RET_CHUNK)
        cos = cos_ref[rows, :]
        sin = sin_ref[rows, :]
        for h in range(RET_HEADS):
            q = z_ref[rows, q0 + h * RET_QK_DIM:q0 + (h + 1) * RET_QK_DIM].astype(F32)
            k = z_ref[rows, k0 + h * RET_QK_DIM:k0 + (h + 1) * RET_QK_DIM].astype(F32)
            v = z_ref[rows, v0 + h * RET_V_DIM:v0 + (h + 1) * RET_V_DIM]
            gate = z_ref[rows, g0 + h * RET_V_DIM:g0 + (h + 1) * RET_V_DIM].astype(F32)
            qr = q * cos + pltpu.roll(q, half, 1) * sin
            kr = (k * cos + pltpu.roll(k, half, 1) * sin) * (RET_QK_DIM ** -0.5)
            scores = lax.dot_general(qr.astype(BF16), kr.astype(BF16), (((1,), (1,)), ((), ())),
                                     preferred_element_type=F32) * intra_ref[h]
            inner = jnp.dot(scores.astype(BF16), v, preferred_element_type=F32)
            st = state[h]
            cross = jnp.dot((qr * qdec_ref[h]).astype(BF16), st.astype(BF16), preferred_element_type=F32)
            kv = lax.dot_general((kr * kdec_ref[h]).astype(BF16), v, (((0,), (0,)), ((), ())),
                                 preferred_element_type=F32)
            state[h] = st * cd_ref[h] + kv
            vcols = slice(h * RET_V_DIM, (h + 1) * RET_V_DIM)
            r = _layer_norm(inner + cross, gn_ref[:, vcols])
            b_ref[rows, vcols] = (r * _silu(gate)).astype(b_ref.dtype)
        return carry

    lax.fori_loop(0, tile // RET_CHUNK, chunk_body, 0)


def _retention_tables(seq):
    half = RET_QK_DIM // 2
    inv = 1.0 / (10000.0 ** (jnp.arange(half, dtype=F32) / half))
    ang = jnp.arange(seq, dtype=F32)[:, None] * inv[None, :]
    cos, sin = jnp.cos(ang), jnp.sin(ang)
    cos_full = jnp.concatenate([cos, cos], axis=-1)
    sin_signed = jnp.concatenate([-sin, sin], axis=-1)
    c = RET_CHUNK
    log_g = jnp.log(1.0 - 2.0 ** (-5.0 - jnp.arange(RET_HEADS, dtype=F32)))
    idx = jnp.arange(c, dtype=F32)
    diff = idx[:, None] - idx[None, :]
    intra = jnp.where(diff >= 0, jnp.exp(log_g[:, None, None] * jnp.maximum(diff, 0.0)), 0.0)
    q_dec = jnp.exp(log_g[:, None] * (idx[None, :] + 1.0))
    k_dec = jnp.exp(log_g[:, None] * (c - 1.0 - idx[None, :]))
    chunk_dec = jnp.exp(log_g * c)
    q_dec = jnp.broadcast_to(q_dec[:, :, None], (RET_HEADS, c, RET_QK_DIM))
    k_dec = jnp.broadcast_to(k_dec[:, :, None], (RET_HEADS, c, RET_QK_DIM))
    return cos_full, sin_signed, intra, q_dec, k_dec, chunk_dec


def pool_retention(z, pool_w, pool_scale, ret_gn_g, *, batch, seq, tile=512):
    m, width = z.shape
    nt = seq // tile
    cos, sin, intra, q_dec, k_dec, chunk_dec = _retention_tables(seq)
    vw = RET_HEADS * RET_V_DIM
    const3 = lambda b, t: (0, 0, 0)
    return pl.pallas_call(
        functools.partial(_pool_retention_kernel, tile=tile),
        grid=(batch, nt),
        in_specs=[pl.BlockSpec(memory_space=pltpu.SMEM),
                  pl.BlockSpec((tile, width), lambda b, t: (b * nt + t, 0)),
                  pl.BlockSpec((tile, RET_QK_DIM), lambda b, t: (t, 0)),
                  pl.BlockSpec((tile, RET_QK_DIM), lambda b, t: (t, 0)),
                  pl.BlockSpec(intra.shape, const3),
                  pl.BlockSpec(q_dec.shape, const3),
                  pl.BlockSpec(k_dec.shape, const3),
                  pl.BlockSpec(pool_w.shape, const3),
                  pl.BlockSpec((1, POOL_WIDTH), lambda b, t: (0, 0)),
                  pl.BlockSpec((1, vw), lambda b, t: (0, 0))],
        out_specs=[pl.BlockSpec((tile, POOL_WIDTH), lambda b, t: (b * nt + t, 0)),
                   pl.BlockSpec((tile, vw), lambda b, t: (b * nt + t, 0))],
        out_shape=[jax.ShapeDtypeStruct((m, POOL_WIDTH), BF16), jax.ShapeDtypeStruct((m, vw), BF16)],
        scratch_shapes=[pltpu.VMEM((POOL_HALO, POOL_WIDTH), F32),
                        pltpu.VMEM((RET_HEADS, RET_QK_DIM, RET_V_DIM), F32)],
        compiler_params=_params("parallel", "arbitrary"),
        name="pool_retention",
    )(chunk_dec, z, cos, sin, intra, q_dec, k_dec, pool_w.astype(BF16),
      pool_scale.reshape(1, POOL_WIDTH), ret_gn_g.reshape(1, vw))


def _sgu_kernel(z_ref, lng_ref, ws_ref, bs_ref, o_ref, *, tile):
    zc = _gelu_tanh(z_ref[...].astype(F32))
    zu = zc[:, :SGU_WIDTH]
    v = _layer_norm(zc[:, SGU_WIDTH:], lng_ref[...]).astype(BF16)
    row = lax.broadcasted_iota(jnp.int32, (SGU_CHUNK, SGU_CHUNK), 0)
    col = lax.broadcasted_iota(jnp.int32, (SGU_CHUNK, SGU_CHUNK), 1)
    for g in range(SGU_GROUPS):
        wm = jnp.where(row >= col, ws_ref[g], 0.0).astype(BF16)
        bias = bs_ref[g]
        cols = slice(g * SGU_GROUP_DIM, (g + 1) * SGU_GROUP_DIM)
        for c in range(tile // SGU_CHUNK):
            rows = slice(c * SGU_CHUNK, (c + 1) * SGU_CHUNK)
            sv = jnp.dot(wm, v[rows, cols], preferred_element_type=F32) + bias
            o_ref[rows, cols] = (zu[rows, cols] * sv).astype(o_ref.dtype)


def spatial_gating(z, ln_g, w_s, b_s, *, tile=256):
    m = z.shape[0]
    return pl.pallas_call(
        functools.partial(_sgu_kernel, tile=tile),
        grid=(m // tile,),
        in_specs=[pl.BlockSpec((tile, 2 * SGU_WIDTH), lambda i: (i, 0)),
                  pl.BlockSpec((1, SGU_WIDTH), lambda i: (0, 0)),
                  pl.BlockSpec(w_s.shape, lambda i: (0, 0, 0)),
                  pl.BlockSpec((SGU_GROUPS, SGU_CHUNK, 1), lambda i: (0, 0, 0))],
        out_specs=pl.BlockSpec((tile, SGU_WIDTH), lambda i: (i, 0)),
        out_shape=jax.ShapeDtypeStruct((m, SGU_WIDTH), BF16),
        compiler_params=_params("parallel"),
        name="spatial_gating",
    )(z, ln_g.reshape(1, SGU_WIDTH), w_s, b_s.reshape(SGU_GROUPS, SGU_CHUNK, 1))


def _t5_bucket(rel):
    n = jnp.maximum(rel, 0)
    max_exact = N_BUCKETS // 2
    large = max_exact + (jnp.log(jnp.maximum(n, 1).astype(F32) / max_exact)
                         / math.log(MAX_DISTANCE / max_exact) * (N_BUCKETS - max_exact)).astype(jnp.int32)
    large = jnp.minimum(large, N_BUCKETS - 1)
    return jnp.where(n < max_exact, n, large)


def _bias_tiles_kernel(relb_ref, bucket_ref, o_ref):
    h = pl.program_id(0)
    bucket = bucket_ref[...]
    acc = jnp.zeros(bucket.shape, F32)
    for b in range(N_BUCKETS):
        acc = jnp.where(bucket == b, relb_ref[b, h], acc)
    o_ref[0] = acc


def rel_bias_tiles(rel_bias, *, tile):
    qk = jnp.arange(tile)[:, None] - jnp.arange(tile)[None, :]
    bucket = jnp.stack([_t5_bucket(qk), _t5_bucket(tile + qk)])
    return pl.pallas_call(
        _bias_tiles_kernel,
        grid=(DIFF_HEADS,),
        in_specs=[pl.BlockSpec(memory_space=pltpu.SMEM),
                  pl.BlockSpec((2, tile, tile), lambda h: (0, 0, 0))],
        out_specs=pl.BlockSpec((1, 2, tile, tile), lambda h: (h, 0, 0, 0)),
        out_shape=jax.ShapeDtypeStruct((DIFF_HEADS, 2, tile, tile), F32),
        compiler_params=_params("parallel"),
        name="rel_bias_tiles",
    )(rel_bias, bucket)


def _diff_attn_kernel(relb_ref, lq1_ref, lk1_ref, lq2_ref, lk2_ref, q_ref, k_ref, v_ref, bias_ref, subg_ref,
                      o_ref, *, tile, seq, lam_init):
    h = pl.program_id(1)
    dh = DIFF_HEAD_DIM
    far_bias = relb_ref[N_BUCKETS - 1, h]
    lam = (jnp.exp(jnp.sum(lq1_ref[...] * lk1_ref[...], axis=-1, keepdims=True))
           - jnp.exp(jnp.sum(lq2_ref[...] * lk2_ref[...], axis=-1, keepdims=True)) + lam_init)
    row = lax.broadcasted_iota(jnp.int32, (tile, tile), 0)
    col = lax.broadcasted_iota(jnp.int32, (tile, tile), 1)
    causal = row >= col
    for qi in range(seq // tile):
        kv_len = (qi + 1) * tile
        rows = slice(qi * tile, kv_len)
        q = (q_ref[rows, :].astype(F32) * (dh ** -0.5)).astype(BF16)
        streams = []
        for s in range(2):
            sc = lax.dot_general(q[:, s * dh:(s + 1) * dh], k_ref[:kv_len, s * dh:(s + 1) * dh],
                                 (((1,), (1,)), ((), ())), preferred_element_type=F32)
            parts = []
            if qi >= 2:
                parts.append(sc[:, :kv_len - 2 * tile] + far_bias)
            if qi >= 1:
                parts.append(sc[:, kv_len - 2 * tile:kv_len - tile] + bias_ref[0, 1])
            parts.append(jnp.where(causal, sc[:, kv_len - tile:] + bias_ref[0, 0], NEG_INF))
            sc = jnp.concatenate(parts, axis=1) if len(parts) > 1 else parts[0]
            p = jnp.exp(sc - jnp.max(sc, axis=-1, keepdims=True))
            l = jnp.sum(p, axis=-1, keepdims=True)
            streams.append(jnp.dot(p.astype(BF16), v_ref[:kv_len, :], preferred_element_type=F32) / l)
        d = streams[0] - lam * streams[1]
        o_ref[rows, :] = (_rms(d, subg_ref[...]) * (1.0 - lam_init)).astype(o_ref.dtype)


def diff_attention(z, bias_tiles, rel_bias, lq1, lk1, lq2, lk2, subln_g, *, batch, seq, layer_idx, tile):
    m = z.shape[0]
    w = 2 * DIFF_HEAD_DIM
    q_blk = 2 * SGU_WIDTH // w
    k_blk = q_blk + DIFF_HEADS
    v_blk = k_blk + DIFF_HEADS
    lam_init = 0.8 - 0.6 * math.exp(-0.3 * layer_idx)
    vec = lambda a: a.reshape(1, DIFF_HEAD_DIM)
    vec_spec = pl.BlockSpec((1, DIFF_HEAD_DIM), lambda b, h: (0, 0))
    return pl.pallas_call(
        functools.partial(_diff_attn_kernel, tile=tile, seq=seq, lam_init=lam_init),
        grid=(batch, DIFF_HEADS),
        in_specs=[pl.BlockSpec(memory_space=pltpu.SMEM),
                  vec_spec, vec_spec, vec_spec, vec_spec,
                  pl.BlockSpec((seq, w), lambda b, h: (b, q_blk + h)),
                  pl.BlockSpec((seq, w), lambda b, h: (b, k_blk + h)),
                  pl.BlockSpec((seq, w), lambda b, h: (b, v_blk + h)),
                  pl.BlockSpec((1, 2, tile, tile), lambda b, h: (h, 0, 0, 0)),
                  pl.BlockSpec((1, DIFF_V_DIM), lambda b, h: (0, 0))],
        out_specs=pl.BlockSpec((seq, DIFF_V_DIM), lambda b, h: (b, h)),
        out_shape=jax.ShapeDtypeStruct((m, DIFF_HEADS * DIFF_V_DIM), BF16),
        compiler_params=_params("parallel", "parallel"),
        name="diff_attention",
    )(rel_bias, vec(lq1), vec(lk1), vec(lq2), vec(lk2), z, z, z, bias_tiles, subln_g.reshape(1, DIFF_V_DIM))


ATTN_TILE = 256
ROW_TILE_OUT = 512
ROW_TILE_DOWN = 256


def kernel(x, w_in_even, w_out_even, pool_w, pool_scale, ret_gn_g, w_in_odd, w_out_odd, sgu_ln_g, sgu_w, sgu_b,
           lam_q1, lam_k1, lam_q2, lam_k2, diff_subln_g, rel_bias, mix_norm_g, ffn_norm_g, w_up, conv_w, conv_b,
           w_down, final_norm_g):
    batch, seq, d = x.shape
    depth = mix_norm_g.shape[0]
    h = x.reshape(batch * seq, d)
    hn = rmsnorm(h, mix_norm_g[0])
    bias_tiles = rel_bias_tiles(rel_bias, tile=ATTN_TILE)
    out = None
    for i in range(depth):
        if i % 2 == 0:
            e = i // 2
            z = matmul(hn, w_in_even[e].astype(BF16))
            pieces = pool_retention(z, pool_w[e], pool_scale[e], ret_gn_g[e], batch=batch, seq=seq)
            w_out = w_out_even[e]
        else:
            o = i // 2
            z = matmul(hn, w_in_odd[o].astype(BF16))
            c_out = spatial_gating(z, sgu_ln_g[o], sgu_w[o], sgu_b[o])
            d_out = diff_attention(z, bias_tiles, rel_bias, lam_q1[o], lam_k1[o], lam_q2[o], lam_k2[o],
                                   diff_subln_g[o], batch=batch, seq=seq, layer_idx=i, tile=ATTN_TILE)
            pieces = [c_out, d_out]
            w_out = w_out_odd[o]
        h, hn = proj_res_norm(pieces, w_out.astype(BF16), h, ffn_norm_g[i], tm=ROW_TILE_OUT)
        act = up_conv_gate(hn, w_up[i].astype(BF16), conv_w[i], conv_b[i], seq=seq)
        if i + 1 < depth:
            h, hn = proj_res_norm([act], w_down[i].astype(BF16), h, mix_norm_g[i + 1], tm=ROW_TILE_DOWN)
        else:
            out = proj_res_norm([act], w_down[i].astype(BF16), h, final_norm_g, tm=ROW_TILE_DOWN,
                                norm_dtype=x.dtype, emit_h=False)
    return out.reshape(batch, seq, d)
```

```python
import functools
import math

import jax
import jax.numpy as jnp
from jax import lax
from jax.experimental import pallas as pl
from jax.experimental.pallas import tpu as pltpu

EPS = 1e-6
POOL_WINDOWS = (2, 4, 8, 16)
POOL_GROUP = 128
POOL_WIDTH = len(POOL_WINDOWS) * POOL_GROUP
POOL_HALO = 16
RET_HEADS = 6
RET_QK_DIM = 128
RET_V_DIM = 256
RET_CHUNK = 128
SGU_CHUNK = 128
SGU_GROUPS = 8
SGU_GROUP_DIM = 128
SGU_WIDTH = SGU_GROUPS * SGU_GROUP_DIM
DIFF_HEADS = 4
DIFF_HEAD_DIM = 128
DIFF_V_DIM = 2 * DIFF_HEAD_DIM
N_BUCKETS = 32
MAX_DISTANCE = 128
CONV_WIDTH = 3
NEG_INF = -1e30

V7X_VMEM_BYTES = 64 * 1024 * 1024
VMEM_LIMIT = V7X_VMEM_BYTES - 8 * 1024 * 1024
SUBLANES = 8

F32 = jnp.float32
BF16 = jnp.bfloat16


def _params(*semantics):
    return pltpu.CompilerParams(dimension_semantics=semantics, vmem_limit_bytes=VMEM_LIMIT)


def _rms(x, g):
    return x * lax.rsqrt(jnp.mean(x * x, axis=-1, keepdims=True) + EPS) * g


def _layer_norm(x, g):
    mu = jnp.mean(x, axis=-1, keepdims=True)
    xc = x - mu
    var = jnp.mean(xc * xc, axis=-1, keepdims=True)
    return xc * lax.rsqrt(var + EPS) * g


def _silu(x):
    return x * (1.0 / (1.0 + jnp.exp(-x)))


def _gelu_tanh(x):
    c = math.sqrt(2.0 / math.pi)
    return 0.5 * x * (1.0 + jnp.tanh(c * (x + 0.044715 * (x * x * x))))


def _rmsnorm_kernel(x_ref, g_ref, o_ref):
    o_ref[...] = _rms(x_ref[...], g_ref[...]).astype(o_ref.dtype)


def rmsnorm(x, g, *, tm=512, out_dtype=BF16):
    m, d = x.shape
    return pl.pallas_call(
        _rmsnorm_kernel,
        grid=(m // tm,),
        in_specs=[pl.BlockSpec((tm, d), lambda i: (i, 0)),
                  pl.BlockSpec((1, d), lambda i: (0, 0))],
        out_specs=pl.BlockSpec((tm, d), lambda i: (i, 0)),
        out_shape=jax.ShapeDtypeStruct((m, d), out_dtype),
        compiler_params=_params("parallel"),
        name="rmsnorm",
    )(x, g.reshape(1, d))


def _matmul_kernel(x_ref, w_ref, o_ref, w_bf):
    @pl.when(pl.program_id(1) == 0)
    def _():
        w_bf[...] = w_ref[...].astype(BF16)

    o_ref[...] = jnp.dot(x_ref[...], w_bf[...], preferred_element_type=F32).astype(o_ref.dtype)


def matmul(x, w_stack, layer, *, tm=1024, tn=1024, out_dtype=BF16):
    m, k = x.shape
    n = w_stack.shape[2]
    return pl.pallas_call(
        _matmul_kernel,
        grid=(n // tn, m // tm),
        in_specs=[pl.BlockSpec((tm, k), lambda j, i: (i, 0)),
                  pl.BlockSpec((None, k, tn), lambda j, i: (layer, 0, j))],
        out_specs=pl.BlockSpec((tm, tn), lambda j, i: (i, j)),
        out_shape=jax.ShapeDtypeStruct((m, n), out_dtype),
        scratch_shapes=[pltpu.VMEM((k, tn), BF16)],
        compiler_params=_params("parallel", "arbitrary"),
        name="in_proj",
    )(x, w_stack)


def _proj_res_norm_kernel(*refs, n_pieces, emit_h):
    x_refs = refs[:n_pieces]
    w_ref, h_ref, g_ref = refs[n_pieces:n_pieces + 3]
    out_refs = refs[n_pieces + 3:]
    if n_pieces > 1:
        x = jnp.concatenate([r[...] for r in x_refs], axis=-1)
    else:
        x = x_refs[0][...]
    h_new = h_ref[...] + jnp.dot(x, w_ref[...], preferred_element_type=F32)
    if emit_h:
        out_refs[0][...] = h_new
    out_refs[-1][...] = _rms(h_new, g_ref[...]).astype(out_refs[-1].dtype)


def proj_res_norm(pieces, w, h, g, *, tm, norm_dtype=BF16, emit_h=True):
    m, d = h.shape
    k = w.shape[0]
    assert sum(p.shape[1] for p in pieces) == k
    in_specs = [pl.BlockSpec((tm, p.shape[1]), lambda i: (i, 0)) for p in pieces]
    in_specs += [pl.BlockSpec((k, d), lambda i: (0, 0), pipeline_mode=pl.Buffered(1)),
                 pl.BlockSpec((tm, d), lambda i: (i, 0)),
                 pl.BlockSpec((1, d), lambda i: (0, 0))]
    row_spec = pl.BlockSpec((tm, d), lambda i: (i, 0))
    norm_shape = jax.ShapeDtypeStruct((m, d), norm_dtype)
    if emit_h:
        out_specs, out_shape = [row_spec, row_spec], [jax.ShapeDtypeStruct((m, d), F32), norm_shape]
    else:
        out_specs, out_shape = row_spec, norm_shape
    return pl.pallas_call(
        functools.partial(_proj_res_norm_kernel, n_pieces=len(pieces), emit_h=emit_h),
        grid=(m // tm,),
        in_specs=in_specs,
        out_specs=out_specs,
        out_shape=out_shape,
        compiler_params=_params("parallel"),
        name="proj_res_norm",
    )(*pieces, w, h, g.reshape(1, d))


def _shift_rows(a, prev, shift):
    rolled = pltpu.roll(a, shift, 0)
    row = lax.broadcasted_iota(jnp.int32, (SUBLANES, a.shape[1]), 0)
    head = rolled[:SUBLANES]
    for r in range(shift):
        head = jnp.where(row == r, prev[SUBLANES - shift + r:SUBLANES - shift + r + 1], head)
    return jnp.concatenate([head, rolled[SUBLANES:]], axis=0)


def _up_conv_gate_kernel(x_ref, wg_ref, wv_ref, cwg_ref, cwv_ref, cbg_ref, cbv_ref, o_ref,
                         wg_bf, wv_bf, carry, *, tiles_per_seq):
    i = pl.program_id(1)

    @pl.when(i == 0)
    def _():
        wg_bf[...] = wg_ref[...].astype(BF16)
        wv_bf[...] = wv_ref[...].astype(BF16)

    @pl.when(i % tiles_per_seq == 0)
    def _():
        carry[...] = jnp.zeros_like(carry)

    x = x_ref[...]

    def conv_branch(idx, w_bf, cw_ref, cb_ref):
        a = jnp.dot(x, w_bf[...], preferred_element_type=F32)
        prev = carry[idx]
        carry[idx] = a[a.shape[0] - SUBLANES:]
        cw = cw_ref[...]
        return (a * cw[2:3] + _shift_rows(a, prev, 1) * cw[1:2]
                + _shift_rows(a, prev, 2) * cw[0:1] + cb_ref[...])

    gate = conv_branch(0, wg_bf, cwg_ref, cbg_ref)
    val = conv_branch(1, wv_bf, cwv_ref, cbv_ref)
    o_ref[...] = (_silu(gate) * val).astype(o_ref.dtype)


def up_conv_gate(x, w_up_stack, layer, conv_w, conv_b, *, seq, tm=1024, tn=512):
    m, k = x.shape
    d_ff = w_up_stack.shape[2] // 2
    nj = d_ff // tn
    conv_b = conv_b.reshape(1, 2 * d_ff)
    return pl.pallas_call(
        functools.partial(_up_conv_gate_kernel, tiles_per_seq=seq // tm),
        grid=(nj, m // tm),
        in_specs=[pl.BlockSpec((tm, k), lambda j, i: (i, 0)),
                  pl.BlockSpec((None, k, tn), lambda j, i: (layer, 0, j)),
                  pl.BlockSpec((None, k, tn), lambda j, i: (layer, 0, nj + j)),
                  pl.BlockSpec((CONV_WIDTH, tn), lambda j, i: (0, j)),
                  pl.BlockSpec((CONV_WIDTH, tn), lambda j, i: (0, nj + j)),
                  pl.BlockSpec((1, tn), lambda j, i: (0, j)),
                  pl.BlockSpec((1, tn), lambda j, i: (0, nj + j))],
        out_specs=pl.BlockSpec((tm, tn), lambda j, i: (i, j)),
        out_shape=jax.ShapeDtypeStruct((m, d_ff), BF16),
        scratch_shapes=[pltpu.VMEM((k, tn), BF16), pltpu.VMEM((k, tn), BF16),
                        pltpu.VMEM((2, SUBLANES, tn), F32)],
        compiler_params=_params("parallel", "arbitrary"),
        name="up_conv_gate",
    )(x, w_up_stack, w_up_stack, conv_w, conv_w, conv_b, conv_b)


def _pool_retention_kernel(cd_ref, z_ref, cos_ref, sin_ref, intra_ref, qdec_ref, kdec_ref, poolw_ref,
                           pscale_ref, gn_ref, a_ref, b_ref, u_carry, state, *, tile):
    t = pl.program_id(1)

    @pl.when(t == 0)
    def _():
        u_carry[...] = jnp.zeros_like(u_carry)
        state[...] = jnp.zeros_like(state)

    u = z_ref[:, :POOL_WIDTH].astype(F32)
    ext = jnp.concatenate([u_carry[...], u], axis=0)
    u_carry[...] = u[tile - POOL_HALO:]
    pos = (t * tile + 1 + lax.broadcasted_iota(jnp.int32, (tile, POOL_GROUP), 0)).astype(F32)
    for gi, w in enumerate(POOL_WINDOWS):
        cols = slice(gi * POOL_GROUP, (gi + 1) * POOL_GROUP)
        s = ext[:, cols]
        span = 1
        while span < w:
            s = s + pltpu.roll(s, span, 0)
            span *= 2
        pooled = s[POOL_HALO:] / jnp.minimum(pos, float(w))
        y = (pooled - u[:, cols]).astype(BF16)
        a = jnp.dot(y, poolw_ref[gi], preferred_element_type=F32)
        a_ref[:, cols] = (a * pscale_ref[:, cols]).astype(a_ref.dtype)

    q0 = POOL_WIDTH
    k0 = q0 + RET_HEADS * RET_QK_DIM
    v0 = k0 + RET_HEADS * RET_QK_DIM
    g0 = v0 + RET_HEADS * RET_V_DIM
    half = RET_QK_DIM // 2

    def chunk_body(c, carry):
        r0 = pl.multiple_of(c * RET_CHUNK, RET_CHUNK)
        rows = pl.ds(r0, RET_CHUNK)
        cos = cos_ref[rows, :]
        sin = sin_ref[rows, :]
        for h in range(RET_HEADS):
            q = z_ref[rows, q0 + h * RET_QK_DIM:q0 + (h + 1) * RET_QK_DIM].astype(F32)
            k = z_ref[rows, k0 + h * RET_QK_DIM:k0 + (h + 1) * RET_QK_DIM].astype(F32)
            v = z_ref[rows, v0 + h * RET_V_DIM:v0 + (h + 1) * RET_V_DIM]
            gate = z_ref[rows, g0 + h * RET_V_DIM:g0 + (h + 1) * RET_V_DIM].astype(F32)
            qr = q * cos + pltpu.roll(q, half, 1) * sin
            kr = (k * cos + pltpu.roll(k, half, 1) * sin) * (RET_QK_DIM ** -0.5)
            scores = lax.dot_general(qr.astype(BF16), kr.astype(BF16), (((1,), (1,)), ((), ())),
                                     preferred_element_type=F32) * intra_ref[h]
            inner = jnp.dot(scores.astype(BF16), v, preferred_element_type=F32)
            st = state[h]
            cross = jnp.dot((qr * qdec_ref[h]).astype(BF16), st.astype(BF16), preferred_element_type=F32)
            kv = lax.dot_general((kr * kdec_ref[h]).astype(BF16), v, (((0,), (0,)), ((), ())),
                                 preferred_element_type=F32)
            state[h] = st * cd_ref[h] + kv
            vcols = slice(h * RET_V_DIM, (h + 1) * RET_V_DIM)
            r = _layer_norm(inner + cross, gn_ref[:, vcols])
            b_ref[rows, vcols] = (r * _silu(gate)).astype(b_ref.dtype)
        return carry

    lax.fori_loop(0, tile // RET_CHUNK, chunk_body, 0)


def _retention_tables(seq):
    half = RET_QK_DIM // 2
    inv = 1.0 / (10000.0 ** (jnp.arange(half, dtype=F32) / half))
    ang = jnp.arange(seq, dtype=F32)[:, None] * inv[None, :]
    cos, sin = jnp.cos(ang), jnp.sin(ang)
    cos_full = jnp.concatenate([cos, cos], axis=-1)
    sin_signed = jnp.concatenate([-sin, sin], axis=-1)
    c = RET_CHUNK
    log_g = jnp.log(1.0 - 2.0 ** (-5.0 - jnp.arange(RET_HEADS, dtype=F32)))
    idx = jnp.arange(c, dtype=F32)
    diff = idx[:, None] - idx[None, :]
    intra = jnp.where(diff >= 0, jnp.exp(log_g[:, None, None] * jnp.maximum(diff, 0.0)), 0.0)
    q_dec = jnp.exp(log_g[:, None] * (idx[None, :] + 1.0))
    k_dec = jnp.exp(log_g[:, None] * (c - 1.0 - idx[None, :]))
    chunk_dec = jnp.exp(log_g * c)
    q_dec = jnp.broadcast_to(q_dec[:, :, None], (RET_HEADS, c, RET_QK_DIM))
    k_dec = jnp.broadcast_to(k_dec[:, :, None], (RET_HEADS, c, RET_QK_DIM))
    return cos_full, sin_signed, intra, q_dec, k_dec, chunk_dec


def pool_retention(z, pool_w, pool_scale, ret_gn_g, *, batch, seq, tile=512):
    m, width = z.shape
    nt = seq // tile
    cos, sin, intra, q_dec, k_dec, chunk_dec = _retention_tables(seq)
    vw = RET_HEADS * RET_V_DIM
    const3 = lambda b, t: (0, 0, 0)
    return pl.pallas_call(
        functools.partial(_pool_retention_kernel, tile=tile),
        grid=(batch, nt),
        in_specs=[pl.BlockSpec(memory_space=pltpu.SMEM),
                  pl.BlockSpec((tile, width), lambda b, t: (b * nt + t, 0)),
                  pl.BlockSpec((tile, RET_QK_DIM), lambda b, t: (t, 0)),
                  pl.BlockSpec((tile, RET_QK_DIM), lambda b, t: (t, 0)),
                  pl.BlockSpec(intra.shape, const3),
                  pl.BlockSpec(q_dec.shape, const3),
                  pl.BlockSpec(k_dec.shape, const3),
                  pl.BlockSpec(pool_w.shape, const3),
                  pl.BlockSpec((1, POOL_WIDTH), lambda b, t: (0, 0)),
                  pl.BlockSpec((1, vw), lambda b, t: (0, 0))],
        out_specs=[pl.BlockSpec((tile, POOL_WIDTH), lambda b, t: (b * nt + t, 0)),
                   pl.BlockSpec((tile, vw), lambda b, t: (b * nt + t, 0))],
        out_shape=[jax.ShapeDtypeStruct((m, POOL_WIDTH), BF16), jax.ShapeDtypeStruct((m, vw), BF16)],
        scratch_shapes=[pltpu.VMEM((POOL_HALO, POOL_WIDTH), F32),
                        pltpu.VMEM((RET_HEADS, RET_QK_DIM, RET_V_DIM), F32)],
        compiler_params=_params("parallel", "arbitrary"),
        name="pool_retention",
    )(chunk_dec, z, cos, sin, intra, q_dec, k_dec, pool_w.astype(BF16),
      pool_scale.reshape(1, POOL_WIDTH), ret_gn_g.reshape(1, vw))


def _sgu_kernel(z_ref, lng_ref, ws_ref, bs_ref, o_ref, *, tile):
    zc = _gelu_tanh(z_ref[...].astype(F32))
    zu = zc[:, :SGU_WIDTH]
    v = _layer_norm(zc[:, SGU_WIDTH:], lng_ref[...]).astype(BF16)
    row = lax.broadcasted_iota(jnp.int32, (SGU_CHUNK, SGU_CHUNK), 0)
    col = lax.broadcasted_iota(jnp.int32, (SGU_CHUNK, SGU_CHUNK), 1)
    for g in range(SGU_GROUPS):
        wm = jnp.where(row >= col, ws_ref[g], 0.0).astype(BF16)
        bias = bs_ref[g]
        cols = slice(g * SGU_GROUP_DIM, (g + 1) * SGU_GROUP_DIM)
        for c in range(tile // SGU_CHUNK):
            rows = slice(c * SGU_CHUNK, (c + 1) * SGU_CHUNK)
            sv = jnp.dot(wm, v[rows, cols], preferred_element_type=F32) + bias
            o_ref[rows, cols] = (zu[rows, cols] * sv).astype(o_ref.dtype)


def spatial_gating(z, ln_g, w_s, b_s, *, tile=256):
    m = z.shape[0]
    return pl.pallas_call(
        functools.partial(_sgu_kernel, tile=tile),
        grid=(m // tile,),
        in_specs=[pl.BlockSpec((tile, 2 * SGU_WIDTH), lambda i: (i, 0)),
                  pl.BlockSpec((1, SGU_WIDTH), lambda i: (0, 0)),
                  pl.BlockSpec(w_s.shape, lambda i: (0, 0, 0)),
                  pl.BlockSpec((SGU_GROUPS, SGU_CHUNK, 1), lambda i: (0, 0, 0))],
        out_specs=pl.BlockSpec((tile, SGU_WIDTH), lambda i: (i, 0)),
        out_shape=jax.ShapeDtypeStruct((m, SGU_WIDTH), BF16),
        compiler_params=_params("parallel"),
        name="spatial_gating",
    )(z, ln_g.reshape(1, SGU_WIDTH), w_s, b_s.reshape(SGU_GROUPS, SGU_CHUNK, 1))


def _t5_bucket(rel):
    n = jnp.maximum(rel, 0)
    max_exact = N_BUCKETS // 2
    large = max_exact + (jnp.log(jnp.maximum(n, 1).astype(F32) / max_exact)
                         / math.log(MAX_DISTANCE / max_exact) * (N_BUCKETS - max_exact)).astype(jnp.int32)
    large = jnp.minimum(large, N_BUCKETS - 1)
    return jnp.where(n < max_exact, n, large)


def _bias_tiles_kernel(relb_ref, bucket_ref, o_ref):
    h = pl.program_id(0)
    bucket = bucket_ref[...]
    acc = jnp.zeros(bucket.shape, F32)
    for b in range(N_BUCKETS):
        acc = jnp.where(bucket == b, relb_ref[b, h], acc)
    o_ref[0] = acc


def rel_bias_tiles(rel_bias, *, tile):
    qk = jnp.arange(tile)[:, None] - jnp.arange(tile)[None, :]
    bucket = jnp.stack([_t5_bucket(qk), _t5_bucket(tile + qk)])
    return pl.pallas_call(
        _bias_tiles_kernel,
        grid=(DIFF_HEADS,),
        in_specs=[pl.BlockSpec(memory_space=pltpu.SMEM),
                  pl.BlockSpec((2, tile, tile), lambda h: (0, 0, 0))],
        out_specs=pl.BlockSpec((1, 2, tile, tile), lambda h: (h, 0, 0, 0)),
        out_shape=jax.ShapeDtypeStruct((DIFF_HEADS, 2, tile, tile), F32),
        compiler_params=_params("parallel"),
        name="rel_bias_tiles",
    )(rel_bias, bucket)


def _diff_attn_kernel(relb_ref, lq1_ref, lk1_ref, lq2_ref, lk2_ref, q_ref, k_ref, v_ref, bias_ref, subg_ref,
                      o_ref, *, tile, seq, lam_init):
    h = pl.program_id(1)
    dh = DIFF_HEAD_DIM
    far_bias = relb_ref[N_BUCKETS - 1, h]
    lam = (jnp.exp(jnp.sum(lq1_ref[...] * lk1_ref[...], axis=-1, keepdims=True))
           - jnp.exp(jnp.sum(lq2_ref[...] * lk2_ref[...], axis=-1, keepdims=True)) + lam_init)
    row = lax.broadcasted_iota(jnp.int32, (tile, tile), 0)
    col = lax.broadcasted_iota(jnp.int32, (tile, tile), 1)
    causal = row >= col
    for qi in range(seq // tile):
        kv_len = (qi + 1) * tile
        rows = slice(qi * tile, kv_len)
        q = (q_ref[rows, :].astype(F32) * (dh ** -0.5)).astype(BF16)
        streams = []
        for s in range(2):
            sc = lax.dot_general(q[:, s * dh:(s + 1) * dh], k_ref[:kv_len, s * dh:(s + 1) * dh],
                                 (((1,), (1,)), ((), ())), preferred_element_type=F32)
            parts = []
            if qi >= 2:
                parts.append(sc[:, :kv_len - 2 * tile] + far_bias)
            if qi >= 1:
                parts.append(sc[:, kv_len - 2 * tile:kv_len - tile] + bias_ref[0, 1])
            parts.append(jnp.where(causal, sc[:, kv_len - tile:] + bias_ref[0, 0], NEG_INF))
            sc = jnp.concatenate(parts, axis=1) if len(parts) > 1 else parts[0]
            p = jnp.exp(sc - jnp.max(sc, axis=-1, keepdims=True))
            l = jnp.sum(p, axis=-1, keepdims=True)
            streams.append(jnp.dot(p.astype(BF16), v_ref[:kv_len, :], preferred_element_type=F32) / l)
        d = streams[0] - lam * streams[1]
        o_ref[rows, :] = (_rms(d, subg_ref[...]) * (1.0 - lam_init)).astype(o_ref.dtype)


def diff_attention(z, bias_tiles, rel_bias, lq1, lk1, lq2, lk2, subln_g, *, batch, seq, layer_idx, tile):
    m = z.shape[0]
    w = 2 * DIFF_HEAD_DIM
    q_blk = 2 * SGU_WIDTH // w
    k_blk = q_blk + DIFF_HEADS
    v_blk = k_blk + DIFF_HEADS
    lam_init = 0.8 - 0.6 * math.exp(-0.3 * layer_idx)
    vec = lambda a: a.reshape(1, DIFF_HEAD_DIM)
    vec_spec = pl.BlockSpec((1, DIFF_HEAD_DIM), lambda b, h: (0, 0))
    return pl.pallas_call(
        functools.partial(_diff_attn_kernel, tile=tile, seq=seq, lam_init=lam_init),
        grid=(batch, DIFF_HEADS),
        in_specs=[pl.BlockSpec(memory_space=pltpu.SMEM),
                  vec_spec, vec_spec, vec_spec, vec_spec,
                  pl.BlockSpec((seq, w), lambda b, h: (b, q_blk + h)),
                  pl.BlockSpec((seq, w), lambda b, h: (b, k_blk + h)),
                  pl.BlockSpec((seq, w), lambda b, h: (b, v_blk + h)),
                  pl.BlockSpec((1, 2, tile, tile), lambda b, h: (h, 0, 0, 0)),
                  pl.BlockSpec((1, DIFF_V_DIM), lambda b, h: (0, 0))],
        out_specs=pl.BlockSpec((seq, DIFF_V_DIM), lambda b, h: (b, h)),
        out_shape=jax.ShapeDtypeStruct((m, DIFF_HEADS * DIFF_V_DIM), BF16),
        compiler_params=_params("parallel", "parallel"),
        name="diff_attention",
    )(rel_bias, vec(lq1), vec(lk1), vec(lq2), vec(lk2), z, z, z, bias_tiles, subln_g.reshape(1, DIFF_V_DIM))


ATTN_TILE = 256
ROW_TILE_OUT = 512
ROW_TILE_DOWN = 256


def kernel(x, w_in_even, w_out_even, pool_w, pool_scale, ret_gn_g, w_in_odd, w_out_odd, sgu_ln_g, sgu_w, sgu_b,
           lam_q1, lam_k1, lam_q2, lam_k2, diff_subln_g, rel_bias, mix_norm_g, ffn_norm_g, w_up, conv_w, conv_b,
           w_down, final_norm_g):
    batch, seq, d = x.shape
    depth = mix_norm_g.shape[0]
    h = x.reshape(batch * seq, d)
    hn = rmsnorm(h, mix_norm_g[0])
    bias_tiles = rel_bias_tiles(rel_bias, tile=ATTN_TILE)
    out = None
    for i in range(depth):
        if i % 2 == 0:
            e = i // 2
            z = matmul(hn, w_in_even, e)
            pieces = pool_retention(z, pool_w[e], pool_scale[e], ret_gn_g[e], batch=batch, seq=seq)
            w_out = w_out_even[e]
        else:
            o = i // 2
            z = matmul(hn, w_in_odd, o)
            c_out = spatial_gating(z, sgu_ln_g[o], sgu_w[o], sgu_b[o])
            d_out = diff_attention(z, bias_tiles, rel_bias, lam_q1[o], lam_k1[o], lam_q2[o], lam_k2[o],
                                   diff_subln_g[o], batch=batch, seq=seq, layer_idx=i, tile=ATTN_TILE)
            pieces = [c_out, d_out]
            w_out = w_out_odd[o]
        h, hn = proj_res_norm(pieces, w_out.astype(BF16), h, ffn_norm_g[i], tm=ROW_TILE_OUT)
        act = up_conv_gate(hn, w_up, i, conv_w[i], conv_b[i], seq=seq)
        if i + 1 < depth:
            h, hn = proj_res_norm([act], w_down[i].astype(BF16), h, mix_norm_g[i + 1], tm=ROW_TILE_DOWN)
        else:
            out = proj_res_norm([act], w_down[i].astype(BF16), h, final_norm_g, tm=ROW_TILE_DOWN,
                                norm_dtype=x.dtype, emit_h=False)
    return out.reshape(batch, seq, d)
```

```python
import functools
import math

import jax
import jax.numpy as jnp
from jax import lax
from jax.experimental import pallas as pl
from jax.experimental.pallas import tpu as pltpu

EPS = 1e-6
POOL_WINDOWS = (2, 4, 8, 16)
POOL_GROUP = 128
POOL_WIDTH = len(POOL_WINDOWS) * POOL_GROUP
POOL_HALO = 16
RET_HEADS = 6
RET_QK_DIM = 128
RET_V_DIM = 256
RET_CHUNK = 128
SGU_CHUNK = 128
SGU_GROUPS = 8
SGU_GROUP_DIM = 128
SGU_WIDTH = SGU_GROUPS * SGU_GROUP_DIM
DIFF_HEADS = 4
DIFF_HEAD_DIM = 128
DIFF_V_DIM = 2 * DIFF_HEAD_DIM
N_BUCKETS = 32
MAX_DISTANCE = 128
CONV_WIDTH = 3
NEG_INF = -1e30

V7X_VMEM_BYTES = 64 * 1024 * 1024
VMEM_LIMIT = V7X_VMEM_BYTES - 8 * 1024 * 1024
SUBLANES = 8

F32 = jnp.float32
BF16 = jnp.bfloat16


def _params(*semantics):
    return pltpu.CompilerParams(dimension_semantics=semantics, vmem_limit_bytes=VMEM_LIMIT)


def _rms(x, g):
    return x * lax.rsqrt(jnp.mean(x * x, axis=-1, keepdims=True) + EPS) * g


def _layer_norm(x, g):
    mu = jnp.mean(x, axis=-1, keepdims=True)
    xc = x - mu
    var = jnp.mean(xc * xc, axis=-1, keepdims=True)
    return xc * lax.rsqrt(var + EPS) * g


def _silu(x):
    return x * (1.0 / (1.0 + jnp.exp(-x)))


def _gelu_tanh(x):
    c = math.sqrt(2.0 / math.pi)
    return 0.5 * x * (1.0 + jnp.tanh(c * (x + 0.044715 * (x * x * x))))


def _weight_rounding_specs(stacks, n_steps, step_of):
    in_specs, out_specs, out_shapes = [], [], []
    for w, layer in stacks:
        _, k, d = w.shape
        rows = k // n_steps
        assert rows * n_steps == k and rows % SUBLANES == 0
        in_specs.append(pl.BlockSpec((None, rows, d), lambda *g, layer=layer: (layer, step_of(*g), 0)))
        out_specs.append(pl.BlockSpec((rows, d), lambda *g: (step_of(*g), 0)))
        out_shapes.append(jax.ShapeDtypeStruct((k, d), BF16))
    return in_specs, out_specs, out_shapes


def _round_weight_slabs(w_refs, w_bf_refs):
    for w_ref, w_bf_ref in zip(w_refs, w_bf_refs):
        w_bf_ref[...] = w_ref[...].astype(BF16)


def _rmsnorm_kernel(x_ref, g_ref, o_ref):
    o_ref[...] = _rms(x_ref[...], g_ref[...]).astype(o_ref.dtype)


def rmsnorm(x, g, *, tm=512, out_dtype=BF16):
    m, d = x.shape
    return pl.pallas_call(
        _rmsnorm_kernel,
        grid=(m // tm,),
        in_specs=[pl.BlockSpec((tm, d), lambda i: (i, 0)),
                  pl.BlockSpec((1, d), lambda i: (0, 0))],
        out_specs=pl.BlockSpec((tm, d), lambda i: (i, 0)),
        out_shape=jax.ShapeDtypeStruct((m, d), out_dtype),
        compiler_params=_params("parallel"),
        name="rmsnorm",
    )(x, g.reshape(1, d))


def _matmul_kernel(x_ref, w_ref, o_ref, w_bf):
    @pl.when(pl.program_id(1) == 0)
    def _():
        w_bf[...] = w_ref[...].astype(BF16)

    o_ref[...] = jnp.dot(x_ref[...], w_bf[...], preferred_element_type=F32).astype(o_ref.dtype)


def matmul(x, w_stack, layer, *, tm=1024, tn=1024, out_dtype=BF16):
    m, k = x.shape
    n = w_stack.shape[2]
    return pl.pallas_call(
        _matmul_kernel,
        grid=(n // tn, m // tm),
        in_specs=[pl.BlockSpec((tm, k), lambda j, i: (i, 0)),
                  pl.BlockSpec((None, k, tn), lambda j, i: (layer, 0, j))],
        out_specs=pl.BlockSpec((tm, tn), lambda j, i: (i, j)),
        out_shape=jax.ShapeDtypeStruct((m, n), out_dtype),
        scratch_shapes=[pltpu.VMEM((k, tn), BF16)],
        compiler_params=_params("parallel", "arbitrary"),
        name="in_proj",
    )(x, w_stack)


def _proj_res_norm_kernel(*refs, n_pieces, emit_h):
    x_refs = refs[:n_pieces]
    w_ref, h_ref, g_ref = refs[n_pieces:n_pieces + 3]
    out_refs = refs[n_pieces + 3:]
    if n_pieces > 1:
        x = jnp.concatenate([r[...] for r in x_refs], axis=-1)
    else:
        x = x_refs[0][...]
    h_new = h_ref[...] + jnp.dot(x, w_ref[...], preferred_element_type=F32)
    if emit_h:
        out_refs[0][...] = h_new
    out_refs[-1][...] = _rms(h_new, g_ref[...]).astype(out_refs[-1].dtype)


def proj_res_norm(pieces, w, h, g, *, tm, norm_dtype=BF16, emit_h=True):
    m, d = h.shape
    k = w.shape[0]
    assert sum(p.shape[1] for p in pieces) == k
    in_specs = [pl.BlockSpec((tm, p.shape[1]), lambda i: (i, 0)) for p in pieces]
    in_specs += [pl.BlockSpec((k, d), lambda i: (0, 0), pipeline_mode=pl.Buffered(1)),
                 pl.BlockSpec((tm, d), lambda i: (i, 0)),
                 pl.BlockSpec((1, d), lambda i: (0, 0))]
    row_spec = pl.BlockSpec((tm, d), lambda i: (i, 0))
    norm_shape = jax.ShapeDtypeStruct((m, d), norm_dtype)
    if emit_h:
        out_specs, out_shape = [row_spec, row_spec], [jax.ShapeDtypeStruct((m, d), F32), norm_shape]
    else:
        out_specs, out_shape = row_spec, norm_shape
    return pl.pallas_call(
        functools.partial(_proj_res_norm_kernel, n_pieces=len(pieces), emit_h=emit_h),
        grid=(m // tm,),
        in_specs=in_specs,
        out_specs=out_specs,
        out_shape=out_shape,
        compiler_params=_params("parallel"),
        name="proj_res_norm",
    )(*pieces, w, h, g.reshape(1, d))


def _shift_rows(a, prev, shift):
    rolled = pltpu.roll(a, shift, 0)
    row = lax.broadcasted_iota(jnp.int32, (SUBLANES, a.shape[1]), 0)
    head = rolled[:SUBLANES]
    for r in range(shift):
        head = jnp.where(row == r, prev[SUBLANES - shift + r:SUBLANES - shift + r + 1], head)
    return jnp.concatenate([head, rolled[SUBLANES:]], axis=0)


def _up_conv_gate_kernel(x_ref, wg_ref, wv_ref, cwg_ref, cwv_ref, cbg_ref, cbv_ref, o_ref,
                         wg_bf, wv_bf, carry, *, tiles_per_seq):
    i = pl.program_id(1)

    @pl.when(i == 0)
    def _():
        wg_bf[...] = wg_ref[...].astype(BF16)
        wv_bf[...] = wv_ref[...].astype(BF16)

    @pl.when(i % tiles_per_seq == 0)
    def _():
        carry[...] = jnp.zeros_like(carry)

    x = x_ref[...]

    def conv_branch(idx, w_bf, cw_ref, cb_ref):
        a = jnp.dot(x, w_bf[...], preferred_element_type=F32)
        prev = carry[idx]
        carry[idx] = a[a.shape[0] - SUBLANES:]
        cw = cw_ref[...]
        return (a * cw[2:3] + _shift_rows(a, prev, 1) * cw[1:2]
                + _shift_rows(a, prev, 2) * cw[0:1] + cb_ref[...])

    gate = conv_branch(0, wg_bf, cwg_ref, cbg_ref)
    val = conv_branch(1, wv_bf, cwv_ref, cbv_ref)
    o_ref[...] = (_silu(gate) * val).astype(o_ref.dtype)


def up_conv_gate(x, w_up_stack, layer, conv_w, conv_b, *, seq, tm=1024, tn=512):
    m, k = x.shape
    d_ff = w_up_stack.shape[2] // 2
    nj = d_ff // tn
    conv_b = conv_b.reshape(1, 2 * d_ff)
    return pl.pallas_call(
        functools.partial(_up_conv_gate_kernel, tiles_per_seq=seq // tm),
        grid=(nj, m // tm),
        in_specs=[pl.BlockSpec((tm, k), lambda j, i: (i, 0)),
                  pl.BlockSpec((None, k, tn), lambda j, i: (layer, 0, j)),
                  pl.BlockSpec((None, k, tn), lambda j, i: (layer, 0, nj + j)),
                  pl.BlockSpec((CONV_WIDTH, tn), lambda j, i: (0, j)),
                  pl.BlockSpec((CONV_WIDTH, tn), lambda j, i: (0, nj + j)),
                  pl.BlockSpec((1, tn), lambda j, i: (0, j)),
                  pl.BlockSpec((1, tn), lambda j, i: (0, nj + j))],
        out_specs=pl.BlockSpec((tm, tn), lambda j, i: (i, j)),
        out_shape=jax.ShapeDtypeStruct((m, d_ff), BF16),
        scratch_shapes=[pltpu.VMEM((k, tn), BF16), pltpu.VMEM((k, tn), BF16),
                        pltpu.VMEM((2, SUBLANES, tn), F32)],
        compiler_params=_params("parallel", "arbitrary"),
        name="up_conv_gate",
    )(x, w_up_stack, w_up_stack, conv_w, conv_w, conv_b, conv_b)


def _pool_retention_kernel(cd_ref, z_ref, cos_ref, sin_ref, intra_ref, qdec_ref, kdec_ref, poolw_ref,
                           pscale_ref, gn_ref, *rest, tile, n_weights):
    w_refs, (a_ref, b_ref), w_bf_refs = rest[:n_weights], rest[n_weights:n_weights + 2], rest[n_weights + 2:-2]
    u_carry, state = rest[-2:]
    _round_weight_slabs(w_refs, w_bf_refs)
    t = pl.program_id(1)

    @pl.when(t == 0)
    def _():
        u_carry[...] = jnp.zeros_like(u_carry)
        state[...] = jnp.zeros_like(state)

    u = z_ref[:, :POOL_WIDTH].astype(F32)
    ext = jnp.concatenate([u_carry[...], u], axis=0)
    u_carry[...] = u[tile - POOL_HALO:]
    pos = (t * tile + 1 + lax.broadcasted_iota(jnp.int32, (tile, POOL_GROUP), 0)).astype(F32)
    for gi, w in enumerate(POOL_WINDOWS):
        cols = slice(gi * POOL_GROUP, (gi + 1) * POOL_GROUP)
        s = ext[:, cols]
        span = 1
        while span < w:
            s = s + pltpu.roll(s, span, 0)
            span *= 2
        pooled = s[POOL_HALO:] / jnp.minimum(pos, float(w))
        y = (pooled - u[:, cols]).astype(BF16)
        a = jnp.dot(y, poolw_ref[gi], preferred_element_type=F32)
        a_ref[:, cols] = (a * pscale_ref[:, cols]).astype(a_ref.dtype)

    q0 = POOL_WIDTH
    k0 = q0 + RET_HEADS * RET_QK_DIM
    v0 = k0 + RET_HEADS * RET_QK_DIM
    g0 = v0 + RET_HEADS * RET_V_DIM
    half = RET_QK_DIM // 2

    def chunk_body(c, carry):
        r0 = pl.multiple_of(c * RET_CHUNK, RET_CHUNK)
        rows = pl.ds(r0, RET_CHUNK)
        cos = cos_ref[rows, :]
        sin = sin_ref[rows, :]
        for h in range(RET_HEADS):
            q = z_ref[rows, q0 + h * RET_QK_DIM:q0 + (h + 1) * RET_QK_DIM].astype(F32)
            k = z_ref[rows, k0 + h * RET_QK_DIM:k0 + (h + 1) * RET_QK_DIM].astype(F32)
            v = z_ref[rows, v0 + h * RET_V_DIM:v0 + (h + 1) * RET_V_DIM]
            gate = z_ref[rows, g0 + h * RET_V_DIM:g0 + (h + 1) * RET_V_DIM].astype(F32)
            qr = q * cos + pltpu.roll(q, half, 1) * sin
            kr = (k * cos + pltpu.roll(k, half, 1) * sin) * (RET_QK_DIM ** -0.5)
            scores = lax.dot_general(qr.astype(BF16), kr.astype(BF16), (((1,), (1,)), ((), ())),
                                     preferred_element_type=F32) * intra_ref[h]
            inner = jnp.dot(scores.astype(BF16), v, preferred_element_type=F32)
            st = state[h]
            cross = jnp.dot((qr * qdec_ref[h]).astype(BF16), st.astype(BF16), preferred_element_type=F32)
            kv = lax.dot_general((kr * kdec_ref[h]).astype(BF16), v, (((0,), (0,)), ((), ())),
                                 preferred_element_type=F32)
            state[h] = st * cd_ref[h] + kv
            vcols = slice(h * RET_V_DIM, (h + 1) * RET_V_DIM)
            r = _layer_norm(inner + cross, gn_ref[:, vcols])
            b_ref[rows, vcols] = (r * _silu(gate)).astype(b_ref.dtype)
        return carry

    lax.fori_loop(0, tile // RET_CHUNK, chunk_body, 0)


def _retention_tables(seq):
    half = RET_QK_DIM // 2
    inv = 1.0 / (10000.0 ** (jnp.arange(half, dtype=F32) / half))
    ang = jnp.arange(seq, dtype=F32)[:, None] * inv[None, :]
    cos, sin = jnp.cos(ang), jnp.sin(ang)
    cos_full = jnp.concatenate([cos, cos], axis=-1)
    sin_signed = jnp.concatenate([-sin, sin], axis=-1)
    c = RET_CHUNK
    log_g = jnp.log(1.0 - 2.0 ** (-5.0 - jnp.arange(RET_HEADS, dtype=F32)))
    idx = jnp.arange(c, dtype=F32)
    diff = idx[:, None] - idx[None, :]
    intra = jnp.where(diff >= 0, jnp.exp(log_g[:, None, None] * jnp.maximum(diff, 0.0)), 0.0)
    q_dec = jnp.exp(log_g[:, None] * (idx[None, :] + 1.0))
    k_dec = jnp.exp(log_g[:, None] * (c - 1.0 - idx[None, :]))
    chunk_dec = jnp.exp(log_g * c)
    q_dec = jnp.broadcast_to(q_dec[:, :, None], (RET_HEADS, c, RET_QK_DIM))
    k_dec = jnp.broadcast_to(k_dec[:, :, None], (RET_HEADS, c, RET_QK_DIM))
    return cos_full, sin_signed, intra, q_dec, k_dec, chunk_dec


def pool_retention(z, pool_w, pool_scale, ret_gn_g, round_weights, *, batch, seq, tile=512):
    m, width = z.shape
    nt = seq // tile
    w_in_specs, w_out_specs, w_out_shapes = _weight_rounding_specs(round_weights, batch * nt, lambda b, t: b * nt + t)
    cos, sin, intra, q_dec, k_dec, chunk_dec = _retention_tables(seq)
    vw = RET_HEADS * RET_V_DIM
    const3 = lambda b, t: (0, 0, 0)
    return pl.pallas_call(
        functools.partial(_pool_retention_kernel, tile=tile, n_weights=len(round_weights)),
        grid=(batch, nt),
        in_specs=[pl.BlockSpec(memory_space=pltpu.SMEM),
                  pl.BlockSpec((tile, width), lambda b, t: (b * nt + t, 0)),
                  pl.BlockSpec((tile, RET_QK_DIM), lambda b, t: (t, 0)),
                  pl.BlockSpec((tile, RET_QK_DIM), lambda b, t: (t, 0)),
                  pl.BlockSpec(intra.shape, const3),
                  pl.BlockSpec(q_dec.shape, const3),
                  pl.BlockSpec(k_dec.shape, const3),
                  pl.BlockSpec(pool_w.shape, const3),
                  pl.BlockSpec((1, POOL_WIDTH), lambda b, t: (0, 0)),
                  pl.BlockSpec((1, vw), lambda b, t: (0, 0))] + w_in_specs,
        out_specs=[pl.BlockSpec((tile, POOL_WIDTH), lambda b, t: (b * nt + t, 0)),
                   pl.BlockSpec((tile, vw), lambda b, t: (b * nt + t, 0))] + w_out_specs,
        out_shape=[jax.ShapeDtypeStruct((m, POOL_WIDTH), BF16), jax.ShapeDtypeStruct((m, vw), BF16)] + w_out_shapes,
        scratch_shapes=[pltpu.VMEM((POOL_HALO, POOL_WIDTH), F32),
                        pltpu.VMEM((RET_HEADS, RET_QK_DIM, RET_V_DIM), F32)],
        compiler_params=_params("parallel", "arbitrary"),
        name="pool_retention",
    )(chunk_dec, z, cos, sin, intra, q_dec, k_dec, pool_w.astype(BF16),
      pool_scale.reshape(1, POOL_WIDTH), ret_gn_g.reshape(1, vw), *[w for w, _ in round_weights])


def _sgu_kernel(z_ref, lng_ref, ws_ref, bs_ref, o_ref, *, tile):
    zc = _gelu_tanh(z_ref[...].astype(F32))
    zu = zc[:, :SGU_WIDTH]
    v = _layer_norm(zc[:, SGU_WIDTH:], lng_ref[...]).astype(BF16)
    row = lax.broadcasted_iota(jnp.int32, (SGU_CHUNK, SGU_CHUNK), 0)
    col = lax.broadcasted_iota(jnp.int32, (SGU_CHUNK, SGU_CHUNK), 1)
    for g in range(SGU_GROUPS):
        wm = jnp.where(row >= col, ws_ref[g], 0.0).astype(BF16)
        bias = bs_ref[g]
        cols = slice(g * SGU_GROUP_DIM, (g + 1) * SGU_GROUP_DIM)
        for c in range(tile // SGU_CHUNK):
            rows = slice(c * SGU_CHUNK, (c + 1) * SGU_CHUNK)
            sv = jnp.dot(wm, v[rows, cols], preferred_element_type=F32) + bias
            o_ref[rows, cols] = (zu[rows, cols] * sv).astype(o_ref.dtype)


def spatial_gating(z, ln_g, w_s, b_s, *, tile=256):
    m = z.shape[0]
    return pl.pallas_call(
        functools.partial(_sgu_kernel, tile=tile),
        grid=(m // tile,),
        in_specs=[pl.BlockSpec((tile, 2 * SGU_WIDTH), lambda i: (i, 0)),
                  pl.BlockSpec((1, SGU_WIDTH), lambda i: (0, 0)),
                  pl.BlockSpec(w_s.shape, lambda i: (0, 0, 0)),
                  pl.BlockSpec((SGU_GROUPS, SGU_CHUNK, 1), lambda i: (0, 0, 0))],
        out_specs=pl.BlockSpec((tile, SGU_WIDTH), lambda i: (i, 0)),
        out_shape=jax.ShapeDtypeStruct((m, SGU_WIDTH), BF16),
        compiler_params=_params("parallel"),
        name="spatial_gating",
    )(z, ln_g.reshape(1, SGU_WIDTH), w_s, b_s.reshape(SGU_GROUPS, SGU_CHUNK, 1))


def _t5_bucket(rel):
    n = jnp.maximum(rel, 0)
    max_exact = N_BUCKETS // 2
    large = max_exact + (jnp.log(jnp.maximum(n, 1).astype(F32) / max_exact)
                         / math.log(MAX_DISTANCE / max_exact) * (N_BUCKETS - max_exact)).astype(jnp.int32)
    large = jnp.minimum(large, N_BUCKETS - 1)
    return jnp.where(n < max_exact, n, large)


def _bias_tiles_kernel(relb_ref, bucket_ref, o_ref):
    h = pl.program_id(0)
    bucket = bucket_ref[...]
    acc = jnp.zeros(bucket.shape, F32)
    for b in range(N_BUCKETS):
        acc = jnp.where(bucket == b, relb_ref[b, h], acc)
    o_ref[0] = acc


def rel_bias_tiles(rel_bias, *, tile):
    qk = jnp.arange(tile)[:, None] - jnp.arange(tile)[None, :]
    bucket = jnp.stack([_t5_bucket(qk), _t5_bucket(tile + qk)])
    return pl.pallas_call(
        _bias_tiles_kernel,
        grid=(DIFF_HEADS,),
        in_specs=[pl.BlockSpec(memory_space=pltpu.SMEM),
                  pl.BlockSpec((2, tile, tile), lambda h: (0, 0, 0))],
        out_specs=pl.BlockSpec((1, 2, tile, tile), lambda h: (h, 0, 0, 0)),
        out_shape=jax.ShapeDtypeStruct((DIFF_HEADS, 2, tile, tile), F32),
        compiler_params=_params("parallel"),
        name="rel_bias_tiles",
    )(rel_bias, bucket)


def _diff_attn_kernel(relb_ref, lq1_ref, lk1_ref, lq2_ref, lk2_ref, q_ref, k_ref, v_ref, bias_ref, subg_ref,
                      *rest, tile, seq, lam_init, n_weights):
    w_refs, o_ref, w_bf_refs = rest[:n_weights], rest[n_weights], rest[n_weights + 1:]
    _round_weight_slabs(w_refs, w_bf_refs)
    h = pl.program_id(1)
    dh = DIFF_HEAD_DIM
    far_bias = relb_ref[N_BUCKETS - 1, h]
    lam = (jnp.exp(jnp.sum(lq1_ref[...] * lk1_ref[...], axis=-1, keepdims=True))
           - jnp.exp(jnp.sum(lq2_ref[...] * lk2_ref[...], axis=-1, keepdims=True)) + lam_init)
    row = lax.broadcasted_iota(jnp.int32, (tile, tile), 0)
    col = lax.broadcasted_iota(jnp.int32, (tile, tile), 1)
    causal = row >= col
    for qi in range(seq // tile):
        kv_len = (qi + 1) * tile
        rows = slice(qi * tile, kv_len)
        q = (q_ref[rows, :].astype(F32) * (dh ** -0.5)).astype(BF16)
        streams = []
        for s in range(2):
            sc = lax.dot_general(q[:, s * dh:(s + 1) * dh], k_ref[:kv_len, s * dh:(s + 1) * dh],
                                 (((1,), (1,)), ((), ())), preferred_element_type=F32)
            parts = []
            if qi >= 2:
                parts.append(sc[:, :kv_len - 2 * tile] + far_bias)
            if qi >= 1:
                parts.append(sc[:, kv_len - 2 * tile:kv_len - tile] + bias_ref[0, 1])
            parts.append(jnp.where(causal, sc[:, kv_len - tile:] + bias_ref[0, 0], NEG_INF))
            sc = jnp.concatenate(parts, axis=1) if len(parts) > 1 else parts[0]
            p = jnp.exp(sc - jnp.max(sc, axis=-1, keepdims=True))
            l = jnp.sum(p, axis=-1, keepdims=True)
            streams.append(jnp.dot(p.astype(BF16), v_ref[:kv_len, :], preferred_element_type=F32) / l)
        d = streams[0] - lam * streams[1]
        o_ref[rows, :] = (_rms(d, subg_ref[...]) * (1.0 - lam_init)).astype(o_ref.dtype)


def diff_attention(z, bias_tiles, rel_bias, lq1, lk1, lq2, lk2, subln_g, round_weights, *, batch, seq, layer_idx,
                   tile):
    m = z.shape[0]
    w_in_specs, w_out_specs, w_out_shapes = _weight_rounding_specs(
        round_weights, batch * DIFF_HEADS, lambda b, h: b * DIFF_HEADS + h)
    w = 2 * DIFF_HEAD_DIM
    q_blk = 2 * SGU_WIDTH // w
    k_blk = q_blk + DIFF_HEADS
    v_blk = k_blk + DIFF_HEADS
    lam_init = 0.8 - 0.6 * math.exp(-0.3 * layer_idx)
    vec = lambda a: a.reshape(1, DIFF_HEAD_DIM)
    vec_spec = pl.BlockSpec((1, DIFF_HEAD_DIM), lambda b, h: (0, 0))
    return pl.pallas_call(
        functools.partial(_diff_attn_kernel, tile=tile, seq=seq, lam_init=lam_init, n_weights=len(round_weights)),
        grid=(batch, DIFF_HEADS),
        in_specs=[pl.BlockSpec(memory_space=pltpu.SMEM),
                  vec_spec, vec_spec, vec_spec, vec_spec,
                  pl.BlockSpec((seq, w), lambda b, h: (b, q_blk + h)),
                  pl.BlockSpec((seq, w), lambda b, h: (b, k_blk + h)),
                  pl.BlockSpec((seq, w), lambda b, h: (b, v_blk + h)),
                  pl.BlockSpec((1, 2, tile, tile), lambda b, h: (h, 0, 0, 0)),
                  pl.BlockSpec((1, DIFF_V_DIM), lambda b, h: (0, 0))] + w_in_specs,
        out_specs=[pl.BlockSpec((seq, DIFF_V_DIM), lambda b, h: (b, h))] + w_out_specs,
        out_shape=[jax.ShapeDtypeStruct((m, DIFF_HEADS * DIFF_V_DIM), BF16)] + w_out_shapes,
        compiler_params=_params("parallel", "parallel"),
        name="diff_attention",
    )(rel_bias, vec(lq1), vec(lk1), vec(lq2), vec(lk2), z, z, z, bias_tiles, subln_g.reshape(1, DIFF_V_DIM),
      *[w for w, _ in round_weights])


ATTN_TILE = 256
ROW_TILE_OUT = 512
ROW_TILE_DOWN = 256


def kernel(x, w_in_even, w_out_even, pool_w, pool_scale, ret_gn_g, w_in_odd, w_out_odd, sgu_ln_g, sgu_w, sgu_b,
           lam_q1, lam_k1, lam_q2, lam_k2, diff_subln_g, rel_bias, mix_norm_g, ffn_norm_g, w_up, conv_w, conv_b,
           w_down, final_norm_g):
    batch, seq, d = x.shape
    depth = mix_norm_g.shape[0]
    h = x.reshape(batch * seq, d)
    hn = rmsnorm(h, mix_norm_g[0])
    bias_tiles = rel_bias_tiles(rel_bias, tile=ATTN_TILE)
    out = None
    for i in range(depth):
        if i % 2 == 0:
            e = i // 2
            z = matmul(hn, w_in_even, e)
            a_out, b_out, w_out, w_dn = pool_retention(z, pool_w[e], pool_scale[e], ret_gn_g[e],
                                                        [(w_out_even, e), (w_down, i)], batch=batch, seq=seq)
            pieces = [a_out, b_out]
        else:
            o = i // 2
            z = matmul(hn, w_in_odd, o)
            c_out = spatial_gating(z, sgu_ln_g[o], sgu_w[o], sgu_b[o])
            d_out, w_out, w_dn = diff_attention(z, bias_tiles, rel_bias, lam_q1[o], lam_k1[o], lam_q2[o], lam_k2[o],
                                                diff_subln_g[o], [(w_out_odd, o), (w_down, i)],
                                                batch=batch, seq=seq, layer_idx=i, tile=ATTN_TILE)
            pieces = [c_out, d_out]
        h, hn = proj_res_norm(pieces, w_out, h, ffn_norm_g[i], tm=ROW_TILE_OUT)
        act = up_conv_gate(hn, w_up, i, conv_w[i], conv_b[i], seq=seq)
        if i + 1 < depth:
            h, hn = proj_res_norm([act], w_dn, h, mix_norm_g[i + 1], tm=ROW_TILE_DOWN)
        else:
            out = proj_res_norm([act], w_dn, h, final_norm_g, tm=ROW_TILE_DOWN,
                                norm_dtype=x.dtype, emit_h=False)
    return out.reshape(batch, seq, d)
```

```python
import functools
import math

import jax
import jax.numpy as jnp
from jax import lax
from jax.experimental import pallas as pl
from jax.experimental.pallas import tpu as pltpu

EPS = 1e-6
POOL_WINDOWS = (2, 4, 8, 16)
POOL_GROUP = 128
POOL_WIDTH = len(POOL_WINDOWS) * POOL_GROUP
POOL_HALO = 16
RET_HEADS = 6
RET_QK_DIM = 128
RET_V_DIM = 256
RET_CHUNK = 128
SGU_CHUNK = 128
SGU_GROUPS = 8
SGU_GROUP_DIM = 128
SGU_WIDTH = SGU_GROUPS * SGU_GROUP_DIM
DIFF_HEADS = 4
DIFF_HEAD_DIM = 128
DIFF_V_DIM = 2 * DIFF_HEAD_DIM
N_BUCKETS = 32
MAX_DISTANCE = 128
CONV_WIDTH = 3
NEG_INF = -1e30
LOG2E = 1.4426950408889634

V7X_VMEM_BYTES = 64 * 1024 * 1024
VMEM_LIMIT = V7X_VMEM_BYTES - 8 * 1024 * 1024
SUBLANES = 8

F32 = jnp.float32
BF16 = jnp.bfloat16


def _params(*semantics):
    return pltpu.CompilerParams(dimension_semantics=semantics, vmem_limit_bytes=VMEM_LIMIT)


def _rms(x, g):
    return x * lax.rsqrt(jnp.mean(x * x, axis=-1, keepdims=True) + EPS) * g


def _layer_norm(x, g):
    mu = jnp.mean(x, axis=-1, keepdims=True)
    xc = x - mu
    var = jnp.mean(xc * xc, axis=-1, keepdims=True)
    return xc * lax.rsqrt(var + EPS) * g


def _silu(x):
    return x * (1.0 / (1.0 + jnp.exp(-x)))


def _gelu_tanh(x):
    k = -2.0 * math.sqrt(2.0 / math.pi) * LOG2E
    return x / (1.0 + jnp.exp2(x * (k + (k * 0.044715) * (x * x))))


def _weight_rounding_specs(stacks, n_steps, step_of):
    in_specs, out_specs, out_shapes = [], [], []
    for w, layer in stacks:
        _, k, d = w.shape
        rows = k // n_steps
        assert rows * n_steps == k and rows % SUBLANES == 0
        in_specs.append(pl.BlockSpec((None, rows, d), lambda *g, layer=layer: (layer, step_of(*g), 0)))
        out_specs.append(pl.BlockSpec((rows, d), lambda *g: (step_of(*g), 0)))
        out_shapes.append(jax.ShapeDtypeStruct((k, d), BF16))
    return in_specs, out_specs, out_shapes


def _round_weight_slabs(w_refs, w_bf_refs):
    for w_ref, w_bf_ref in zip(w_refs, w_bf_refs):
        w_bf_ref[...] = w_ref[...].astype(BF16)


def _rmsnorm_kernel(x_ref, g_ref, o_ref):
    o_ref[...] = _rms(x_ref[...], g_ref[...]).astype(o_ref.dtype)


def rmsnorm(x, g, *, tm=512, out_dtype=BF16):
    m, d = x.shape
    return pl.pallas_call(
        _rmsnorm_kernel,
        grid=(m // tm,),
        in_specs=[pl.BlockSpec((tm, d), lambda i: (i, 0)),
                  pl.BlockSpec((1, d), lambda i: (0, 0))],
        out_specs=pl.BlockSpec((tm, d), lambda i: (i, 0)),
        out_shape=jax.ShapeDtypeStruct((m, d), out_dtype),
        compiler_params=_params("parallel"),
        name="rmsnorm",
    )(x, g.reshape(1, d))


def _matmul_kernel(x_ref, w_ref, o_ref, w_bf):
    @pl.when(pl.program_id(1) == 0)
    def _():
        w_bf[...] = w_ref[...].astype(BF16)

    o_ref[...] = jnp.dot(x_ref[...], w_bf[...], preferred_element_type=F32).astype(o_ref.dtype)


def matmul(x, w_stack, layer, *, tm=2048, tn=1024, out_dtype=BF16):
    m, k = x.shape
    n = w_stack.shape[2]
    return pl.pallas_call(
        _matmul_kernel,
        grid=(n // tn, m // tm),
        in_specs=[pl.BlockSpec((tm, k), lambda j, i: (i, 0)),
                  pl.BlockSpec((None, k, tn), lambda j, i: (layer, 0, j))],
        out_specs=pl.BlockSpec((tm, tn), lambda j, i: (i, j)),
        out_shape=jax.ShapeDtypeStruct((m, n), out_dtype),
        scratch_shapes=[pltpu.VMEM((k, tn), BF16)],
        compiler_params=_params("parallel", "arbitrary"),
        name="in_proj",
    )(x, w_stack)


def _proj_res_norm_kernel(*refs, n_pieces, emit_h):
    x_refs = refs[:n_pieces]
    w_ref, h_ref, g_ref = refs[n_pieces:n_pieces + 3]
    out_refs = refs[n_pieces + 3:]
    if n_pieces > 1:
        x = jnp.concatenate([r[...] for r in x_refs], axis=-1)
    else:
        x = x_refs[0][...]
    h_new = h_ref[...] + jnp.dot(x, w_ref[...], preferred_element_type=F32)
    if emit_h:
        out_refs[0][...] = h_new
    out_refs[-1][...] = _rms(h_new, g_ref[...]).astype(out_refs[-1].dtype)


def proj_res_norm(pieces, w, h, g, *, tm, norm_dtype=BF16, emit_h=True):
    m, d = h.shape
    k = w.shape[0]
    assert sum(p.shape[1] for p in pieces) == k
    in_specs = [pl.BlockSpec((tm, p.shape[1]), lambda i: (i, 0)) for p in pieces]
    in_specs += [pl.BlockSpec((k, d), lambda i: (0, 0), pipeline_mode=pl.Buffered(1)),
                 pl.BlockSpec((tm, d), lambda i: (i, 0)),
                 pl.BlockSpec((1, d), lambda i: (0, 0))]
    row_spec = pl.BlockSpec((tm, d), lambda i: (i, 0))
    norm_shape = jax.ShapeDtypeStruct((m, d), norm_dtype)
    if emit_h:
        out_specs, out_shape = [row_spec, row_spec], [jax.ShapeDtypeStruct((m, d), F32), norm_shape]
    else:
        out_specs, out_shape = row_spec, norm_shape
    return pl.pallas_call(
        functools.partial(_proj_res_norm_kernel, n_pieces=len(pieces), emit_h=emit_h),
        grid=(m // tm,),
        in_specs=in_specs,
        out_specs=out_specs,
        out_shape=out_shape,
        compiler_params=_params("parallel"),
        name="proj_res_norm",
    )(*pieces, w, h, g.reshape(1, d))


def _shift_rows(a, prev, shift):
    rolled = pltpu.roll(a, shift, 0)
    row = lax.broadcasted_iota(jnp.int32, (SUBLANES, a.shape[1]), 0)
    head = rolled[:SUBLANES]
    for r in range(shift):
        head = jnp.where(row == r, prev[SUBLANES - shift + r:SUBLANES - shift + r + 1], head)
    return jnp.concatenate([head, rolled[SUBLANES:]], axis=0)


def _up_conv_gate_kernel(x_ref, wg_ref, wv_ref, cwg_ref, cwv_ref, cbg_ref, cbv_ref, o_ref,
                         wg_bf, wv_bf, carry, *, tiles_per_seq):
    i = pl.program_id(1)

    @pl.when(i == 0)
    def _():
        wg_bf[...] = wg_ref[...].astype(BF16)
        wv_bf[...] = wv_ref[...].astype(BF16)

    @pl.when(i % tiles_per_seq == 0)
    def _():
        carry[...] = jnp.zeros_like(carry)

    x = x_ref[...]

    def conv_branch(idx, w_bf, cw_ref, cb_ref):
        a = jnp.dot(x, w_bf[...], preferred_element_type=F32)
        prev = carry[idx]
        carry[idx] = a[a.shape[0] - SUBLANES:]
        cw = cw_ref[...]
        return (a * cw[2:3] + _shift_rows(a, prev, 1) * cw[1:2]
                + _shift_rows(a, prev, 2) * cw[0:1] + cb_ref[...])

    gate = conv_branch(0, wg_bf, cwg_ref, cbg_ref)
    val = conv_branch(1, wv_bf, cwv_ref, cbv_ref)
    o_ref[...] = (_silu(gate) * val).astype(o_ref.dtype)


def up_conv_gate(x, w_up_stack, layer, conv_w, conv_b, *, seq, tm=1024, tn=512):
    m, k = x.shape
    d_ff = w_up_stack.shape[2] // 2
    nj = d_ff // tn
    conv_b = conv_b.reshape(1, 2 * d_ff)
    return pl.pallas_call(
        functools.partial(_up_conv_gate_kernel, tiles_per_seq=seq // tm),
        grid=(nj, m // tm),
        in_specs=[pl.BlockSpec((tm, k), lambda j, i: (i, 0)),
                  pl.BlockSpec((None, k, tn), lambda j, i: (layer, 0, j)),
                  pl.BlockSpec((None, k, tn), lambda j, i: (layer, 0, nj + j)),
                  pl.BlockSpec((CONV_WIDTH, tn), lambda j, i: (0, j)),
                  pl.BlockSpec((CONV_WIDTH, tn), lambda j, i: (0, nj + j)),
                  pl.BlockSpec((1, tn), lambda j, i: (0, j)),
                  pl.BlockSpec((1, tn), lambda j, i: (0, nj + j))],
        out_specs=pl.BlockSpec((tm, tn), lambda j, i: (i, j)),
        out_shape=jax.ShapeDtypeStruct((m, d_ff), BF16),
        scratch_shapes=[pltpu.VMEM((k, tn), BF16), pltpu.VMEM((k, tn), BF16),
                        pltpu.VMEM((2, SUBLANES, tn), F32)],
        compiler_params=_params("parallel", "arbitrary"),
        name="up_conv_gate",
    )(x, w_up_stack, w_up_stack, conv_w, conv_w, conv_b, conv_b)


def _pool_retention_kernel(cd_ref, z_ref, cos_ref, sin_ref, intra_ref, qdec_ref, kdec_ref, poolw_ref,
                           pscale_ref, gn_ref, *rest, tile, n_weights):
    w_refs, (a_ref, b_ref), w_bf_refs = rest[:n_weights], rest[n_weights:n_weights + 2], rest[n_weights + 2:-2]
    u_carry, state = rest[-2:]
    _round_weight_slabs(w_refs, w_bf_refs)
    t = pl.program_id(1)

    @pl.when(t == 0)
    def _():
        u_carry[...] = jnp.zeros_like(u_carry)
        state[...] = jnp.zeros_like(state)

    u = z_ref[:, :POOL_WIDTH].astype(F32)
    ext = jnp.concatenate([u_carry[...], u], axis=0)
    u_carry[...] = u[tile - POOL_HALO:]
    pos = (t * tile + 1 + lax.broadcasted_iota(jnp.int32, (tile, POOL_GROUP), 0)).astype(F32)
    for gi, w in enumerate(POOL_WINDOWS):
        cols = slice(gi * POOL_GROUP, (gi + 1) * POOL_GROUP)
        s = ext[:, cols]
        span = 1
        while span < w:
            s = s + pltpu.roll(s, span, 0)
            span *= 2
        pooled = s[POOL_HALO:] / jnp.minimum(pos, float(w))
        y = (pooled - u[:, cols]).astype(BF16)
        a = jnp.dot(y, poolw_ref[gi], preferred_element_type=F32)
        a_ref[:, cols] = (a * pscale_ref[:, cols]).astype(a_ref.dtype)

    q0 = POOL_WIDTH
    k0 = q0 + RET_HEADS * RET_QK_DIM
    v0 = k0 + RET_HEADS * RET_QK_DIM
    g0 = v0 + RET_HEADS * RET_V_DIM
    half = RET_QK_DIM // 2

    def chunk_body(c, carry):
        r0 = pl.multiple_of(c * RET_CHUNK, RET_CHUNK)
        rows = pl.ds(r0, RET_CHUNK)
        cos = cos_ref[rows, :]
        sin = sin_ref[rows, :]
        for h in range(RET_HEADS):
            q = z_ref[rows, q0 + h * RET_QK_DIM:q0 + (h + 1) * RET_QK_DIM].astype(F32)
            k = z_ref[rows, k0 + h * RET_QK_DIM:k0 + (h + 1) * RET_QK_DIM].astype(F32)
            v = z_ref[rows, v0 + h * RET_V_DIM:v0 + (h + 1) * RET_V_DIM]
            gate = z_ref[rows, g0 + h * RET_V_DIM:g0 + (h + 1) * RET_V_DIM].astype(F32)
            qr = q * cos + pltpu.roll(q, half, 1) * sin
            kr = (k * cos + pltpu.roll(k, half, 1) * sin) * (RET_QK_DIM ** -0.5)
            scores = lax.dot_general(qr.astype(BF16), kr.astype(BF16), (((1,), (1,)), ((), ())),
                                     preferred_element_type=F32) * intra_ref[h]
            inner = jnp.dot(scores.astype(BF16), v, preferred_element_type=F32)
            st = state[h]
            cross = jnp.dot((qr * qdec_ref[h]).astype(BF16), st.astype(BF16), preferred_element_type=F32)
            kv = lax.dot_general((kr * kdec_ref[h]).astype(BF16), v, (((0,), (0,)), ((), ())),
                                 preferred_element_type=F32)
            state[h] = st * cd_ref[h] + kv
            vcols = slice(h * RET_V_DIM, (h + 1) * RET_V_DIM)
            r = _layer_norm(inner + cross, gn_ref[:, vcols])
            b_ref[rows, vcols] = (r * _silu(gate)).astype(b_ref.dtype)
        return carry

    lax.fori_loop(0, tile // RET_CHUNK, chunk_body, 0)


def _retention_tables(seq):
    half = RET_QK_DIM // 2
    inv = 1.0 / (10000.0 ** (jnp.arange(half, dtype=F32) / half))
    ang = jnp.arange(seq, dtype=F32)[:, None] * inv[None, :]
    cos, sin = jnp.cos(ang), jnp.sin(ang)
    cos_full = jnp.concatenate([cos, cos], axis=-1)
    sin_signed = jnp.concatenate([-sin, sin], axis=-1)
    c = RET_CHUNK
    log_g = jnp.log(1.0 - 2.0 ** (-5.0 - jnp.arange(RET_HEADS, dtype=F32)))
    idx = jnp.arange(c, dtype=F32)
    diff = idx[:, None] - idx[None, :]
    intra = jnp.where(diff >= 0, jnp.exp(log_g[:, None, None] * jnp.maximum(diff, 0.0)), 0.0)
    q_dec = jnp.exp(log_g[:, None] * (idx[None, :] + 1.0))
    k_dec = jnp.exp(log_g[:, None] * (c - 1.0 - idx[None, :]))
    chunk_dec = jnp.exp(log_g * c)
    q_dec = jnp.broadcast_to(q_dec[:, :, None], (RET_HEADS, c, RET_QK_DIM))
    k_dec = jnp.broadcast_to(k_dec[:, :, None], (RET_HEADS, c, RET_QK_DIM))
    return cos_full, sin_signed, intra, q_dec, k_dec, chunk_dec


def pool_retention(z, pool_w, pool_scale, ret_gn_g, round_weights, *, batch, seq, tile=512):
    m, width = z.shape
    nt = seq // tile
    w_in_specs, w_out_specs, w_out_shapes = _weight_rounding_specs(round_weights, batch * nt, lambda b, t: b * nt + t)
    cos, sin, intra, q_dec, k_dec, chunk_dec = _retention_tables(seq)
    vw = RET_HEADS * RET_V_DIM
    const3 = lambda b, t: (0, 0, 0)
    return pl.pallas_call(
        functools.partial(_pool_retention_kernel, tile=tile, n_weights=len(round_weights)),
        grid=(batch, nt),
        in_specs=[pl.BlockSpec(memory_space=pltpu.SMEM),
                  pl.BlockSpec((tile, width), lambda b, t: (b * nt + t, 0)),
                  pl.BlockSpec((tile, RET_QK_DIM), lambda b, t: (t, 0)),
                  pl.BlockSpec((tile, RET_QK_DIM), lambda b, t: (t, 0)),
                  pl.BlockSpec(intra.shape, const3),
                  pl.BlockSpec(q_dec.shape, const3),
                  pl.BlockSpec(k_dec.shape, const3),
                  pl.BlockSpec(pool_w.shape, const3),
                  pl.BlockSpec((1, POOL_WIDTH), lambda b, t: (0, 0)),
                  pl.BlockSpec((1, vw), lambda b, t: (0, 0))] + w_in_specs,
        out_specs=[pl.BlockSpec((tile, POOL_WIDTH), lambda b, t: (b * nt + t, 0)),
                   pl.BlockSpec((tile, vw), lambda b, t: (b * nt + t, 0))] + w_out_specs,
        out_shape=[jax.ShapeDtypeStruct((m, POOL_WIDTH), BF16), jax.ShapeDtypeStruct((m, vw), BF16)] + w_out_shapes,
        scratch_shapes=[pltpu.VMEM((POOL_HALO, POOL_WIDTH), F32),
                        pltpu.VMEM((RET_HEADS, RET_QK_DIM, RET_V_DIM), F32)],
        compiler_params=_params("parallel", "arbitrary"),
        name="pool_retention",
    )(chunk_dec, z, cos, sin, intra, q_dec, k_dec, pool_w.astype(BF16),
      pool_scale.reshape(1, POOL_WIDTH), ret_gn_g.reshape(1, vw), *[w for w, _ in round_weights])


def _sgu_kernel(z_ref, lng_ref, ws_ref, bs_ref, o_ref, *, tile):
    zc = _gelu_tanh(z_ref[...].astype(F32))
    zu = zc[:, :SGU_WIDTH]
    v = _layer_norm(zc[:, SGU_WIDTH:], lng_ref[...]).astype(BF16)
    row = lax.broadcasted_iota(jnp.int32, (SGU_CHUNK, SGU_CHUNK), 0)
    col = lax.broadcasted_iota(jnp.int32, (SGU_CHUNK, SGU_CHUNK), 1)
    for g in range(SGU_GROUPS):
        wm = jnp.where(row >= col, ws_ref[g], 0.0).astype(BF16)
        bias = bs_ref[g]
        cols = slice(g * SGU_GROUP_DIM, (g + 1) * SGU_GROUP_DIM)
        for c in range(tile // SGU_CHUNK):
            rows = slice(c * SGU_CHUNK, (c + 1) * SGU_CHUNK)
            sv = jnp.dot(wm, v[rows, cols], preferred_element_type=F32) + bias
            o_ref[rows, cols] = (zu[rows, cols] * sv).astype(o_ref.dtype)


def spatial_gating(z, ln_g, w_s, b_s, *, tile=256):
    m = z.shape[0]
    return pl.pallas_call(
        functools.partial(_sgu_kernel, tile=tile),
        grid=(m // tile,),
        in_specs=[pl.BlockSpec((tile, 2 * SGU_WIDTH), lambda i: (i, 0)),
                  pl.BlockSpec((1, SGU_WIDTH), lambda i: (0, 0)),
                  pl.BlockSpec(w_s.shape, lambda i: (0, 0, 0)),
                  pl.BlockSpec((SGU_GROUPS, SGU_CHUNK, 1), lambda i: (0, 0, 0))],
        out_specs=pl.BlockSpec((tile, SGU_WIDTH), lambda i: (i, 0)),
        out_shape=jax.ShapeDtypeStruct((m, SGU_WIDTH), BF16),
        compiler_params=_params("parallel"),
        name="spatial_gating",
    )(z, ln_g.reshape(1, SGU_WIDTH), w_s, b_s.reshape(SGU_GROUPS, SGU_CHUNK, 1))


def _t5_bucket(rel):
    n = jnp.maximum(rel, 0)
    max_exact = N_BUCKETS // 2
    large = max_exact + (jnp.log(jnp.maximum(n, 1).astype(F32) / max_exact)
                         / math.log(MAX_DISTANCE / max_exact) * (N_BUCKETS - max_exact)).astype(jnp.int32)
    large = jnp.minimum(large, N_BUCKETS - 1)
    return jnp.where(n < max_exact, n, large)


def _bias_tiles_kernel(relb_ref, bucket_ref, o_ref):
    h = pl.program_id(0)
    bucket = bucket_ref[...]
    acc = jnp.zeros(bucket.shape, F32)
    for b in range(N_BUCKETS):
        acc = jnp.where(bucket == b, relb_ref[b, h], acc)
    far_bias = relb_ref[N_BUCKETS - 1, h]
    o_ref[0] = (acc - far_bias) * LOG2E


def rel_bias_tiles(rel_bias, *, tile):
    qk = jnp.arange(tile)[:, None] - jnp.arange(tile)[None, :]
    bucket = jnp.stack([_t5_bucket(qk), _t5_bucket(tile + qk)])
    return pl.pallas_call(
        _bias_tiles_kernel,
        grid=(DIFF_HEADS,),
        in_specs=[pl.BlockSpec(memory_space=pltpu.SMEM),
                  pl.BlockSpec((2, tile, tile), lambda h: (0, 0, 0))],
        out_specs=pl.BlockSpec((1, 2, tile, tile), lambda h: (h, 0, 0, 0)),
        out_shape=jax.ShapeDtypeStruct((DIFF_HEADS, 2, tile, tile), F32),
        compiler_params=_params("parallel"),
        name="rel_bias_tiles",
    )(rel_bias, bucket)


def _diff_attn_kernel(lq1_ref, lk1_ref, lq2_ref, lk2_ref, q_ref, k_ref, v_ref, bias_ref, subg_ref,
                      *rest, tile, seq, lam_init, n_weights):
    w_refs, o_ref, w_bf_refs = rest[:n_weights], rest[n_weights], rest[n_weights + 1:]
    _round_weight_slabs(w_refs, w_bf_refs)
    dh = DIFF_HEAD_DIM
    lam = (jnp.exp(jnp.sum(lq1_ref[...] * lk1_ref[...], axis=-1, keepdims=True))
           - jnp.exp(jnp.sum(lq2_ref[...] * lk2_ref[...], axis=-1, keepdims=True)) + lam_init)
    row = lax.broadcasted_iota(jnp.int32, (tile, tile), 0)
    col = lax.broadcasted_iota(jnp.int32, (tile, tile), 1)
    causal = row >= col

    def query_tile(qi):
        kv_len = (qi + 1) * tile
        rows = slice(qi * tile, kv_len)
        q = (q_ref[rows, :].astype(F32) * (dh ** -0.5 * LOG2E)).astype(BF16)
        streams = []
        for s in range(2):
            sc = lax.dot_general(q[:, s * dh:(s + 1) * dh], k_ref[:kv_len, s * dh:(s + 1) * dh],
                                 (((1,), (1,)), ((), ())), preferred_element_type=F32)
            parts = []
            if qi >= 2:
                parts.append(sc[:, :kv_len - 2 * tile])
            if qi >= 1:
                parts.append(sc[:, kv_len - 2 * tile:kv_len - tile] + bias_ref[0, 1])
            parts.append(jnp.where(causal, sc[:, kv_len - tile:] + bias_ref[0, 0], NEG_INF))
            sc = jnp.concatenate(parts, axis=1) if len(parts) > 1 else parts[0]
            p = jnp.exp2(sc - jnp.max(sc, axis=-1, keepdims=True))
            l = jnp.sum(p, axis=-1, keepdims=True)
            streams.append(jnp.dot(p.astype(BF16), v_ref[:kv_len, :], preferred_element_type=F32) / l)
        d = streams[0] - lam * streams[1]
        o_ref[rows, :] = (_rms(d, subg_ref[...]) * (1.0 - lam_init)).astype(o_ref.dtype)

    n_tiles = seq // tile
    always = pl.program_id(0) >= 0
    for first in range(n_tiles // 2):
        @pl.when(always)
        def _(first=first):
            query_tile(first)
            query_tile(n_tiles - 1 - first)


def diff_attention(z, bias_tiles, lq1, lk1, lq2, lk2, subln_g, round_weights, *, batch, seq, layer_idx,
                   tile):
    m = z.shape[0]
    w_in_specs, w_out_specs, w_out_shapes = _weight_rounding_specs(
        round_weights, batch * DIFF_HEADS, lambda b, h: b * DIFF_HEADS + h)
    w = 2 * DIFF_HEAD_DIM
    q_blk = 2 * SGU_WIDTH // w
    k_blk = q_blk + DIFF_HEADS
    v_blk = k_blk + DIFF_HEADS
    lam_init = 0.8 - 0.6 * math.exp(-0.3 * layer_idx)
    vec = lambda a: a.reshape(1, DIFF_HEAD_DIM)
    vec_spec = pl.BlockSpec((1, DIFF_HEAD_DIM), lambda b, h: (0, 0))
    return pl.pallas_call(
        functools.partial(_diff_attn_kernel, tile=tile, seq=seq, lam_init=lam_init, n_weights=len(round_weights)),
        grid=(batch, DIFF_HEADS),
        in_specs=[vec_spec, vec_spec, vec_spec, vec_spec,
                  pl.BlockSpec((seq, w), lambda b, h: (b, q_blk + h)),
                  pl.BlockSpec((seq, w), lambda b, h: (b, k_blk + h)),
                  pl.BlockSpec((seq, w), lambda b, h: (b, v_blk + h)),
                  pl.BlockSpec((1, 2, tile, tile), lambda b, h: (h, 0, 0, 0)),
                  pl.BlockSpec((1, DIFF_V_DIM), lambda b, h: (0, 0))] + w_in_specs,
        out_specs=[pl.BlockSpec((seq, DIFF_V_DIM), lambda b, h: (b, h))] + w_out_specs,
        out_shape=[jax.ShapeDtypeStruct((m, DIFF_HEADS * DIFF_V_DIM), BF16)] + w_out_shapes,
        compiler_params=_params("parallel", "parallel"),
        name="diff_attention",
    )(vec(lq1), vec(lk1), vec(lq2), vec(lk2), z, z, z, bias_tiles, subln_g.reshape(1, DIFF_V_DIM),
      *[w for w, _ in round_weights])


ATTN_TILE = 256
ROW_TILE_OUT = 512
ROW_TILE_DOWN = 256


def kernel(x, w_in_even, w_out_even, pool_w, pool_scale, ret_gn_g, w_in_odd, w_out_odd, sgu_ln_g, sgu_w, sgu_b,
           lam_q1, lam_k1, lam_q2, lam_k2, diff_subln_g, rel_bias, mix_norm_g, ffn_norm_g, w_up, conv_w, conv_b,
           w_down, final_norm_g):
    batch, seq, d = x.shape
    depth = mix_norm_g.shape[0]
    h = x.reshape(batch * seq, d)
    hn = rmsnorm(h, mix_norm_g[0])
    bias_tiles = rel_bias_tiles(rel_bias, tile=ATTN_TILE)
    out = None
    for i in range(depth):
        if i % 2 == 0:
            e = i // 2
            z = matmul(hn, w_in_even, e)
            a_out, b_out, w_out, w_dn = pool_retention(z, pool_w[e], pool_scale[e], ret_gn_g[e],
                                                        [(w_out_even, e), (w_down, i)], batch=batch, seq=seq)
            pieces = [a_out, b_out]
        else:
            o = i // 2
            z = matmul(hn, w_in_odd, o)
            c_out = spatial_gating(z, sgu_ln_g[o], sgu_w[o], sgu_b[o])
            d_out, w_out, w_dn = diff_attention(z, bias_tiles, lam_q1[o], lam_k1[o], lam_q2[o], lam_k2[o],
                                                diff_subln_g[o], [(w_out_odd, o), (w_down, i)],
                                                batch=batch, seq=seq, layer_idx=i, tile=ATTN_TILE)
            pieces = [c_out, d_out]
        h, hn = proj_res_norm(pieces, w_out, h, ffn_norm_g[i], tm=ROW_TILE_OUT)
        act = up_conv_gate(hn, w_up, i, conv_w[i], conv_b[i], seq=seq)
        if i + 1 < depth:
            h, hn = proj_res_norm([act], w_dn, h, mix_norm_g[i + 1], tm=ROW_TILE_DOWN)
        else:
            out = proj_res_norm([act], w_dn, h, final_norm_g, tm=ROW_TILE_DOWN,
                                norm_dtype=x.dtype, emit_h=False)
    return out.reshape(batch, seq, d)
```

```python
import functools
import math

import jax
import jax.numpy as jnp
from jax import lax
from jax.experimental import pallas as pl
from jax.experimental.pallas import tpu as pltpu

EPS = 1e-6
POOL_WINDOWS = (2, 4, 8, 16)
POOL_GROUP = 128
POOL_WIDTH = len(POOL_WINDOWS) * POOL_GROUP
POOL_HALO = 16
RET_HEADS = 6
RET_QK_DIM = 128
RET_V_DIM = 256
RET_CHUNK = 128
SGU_CHUNK = 128
SGU_GROUPS = 8
SGU_GROUP_DIM = 128
SGU_WIDTH = SGU_GROUPS * SGU_GROUP_DIM
DIFF_HEADS = 4
DIFF_HEAD_DIM = 128
DIFF_V_DIM = 2 * DIFF_HEAD_DIM
N_BUCKETS = 32
MAX_DISTANCE = 128
CONV_WIDTH = 3
NEG_INF = -1e30
LOG2E = 1.4426950408889634

V7X_VMEM_BYTES = 64 * 1024 * 1024
VMEM_LIMIT = V7X_VMEM_BYTES - 8 * 1024 * 1024
SUBLANES = 8

F32 = jnp.float32
BF16 = jnp.bfloat16


def _params(*semantics):
    return pltpu.CompilerParams(dimension_semantics=semantics, vmem_limit_bytes=VMEM_LIMIT)


def _rms(x, g):
    return x * lax.rsqrt(jnp.mean(x * x, axis=-1, keepdims=True) + EPS) * g


def _layer_norm(x, g):
    mu = jnp.mean(x, axis=-1, keepdims=True)
    xc = x - mu
    var = jnp.mean(xc * xc, axis=-1, keepdims=True)
    return xc * lax.rsqrt(var + EPS) * g


def _silu(x):
    return x * (1.0 / (1.0 + jnp.exp(-x)))


def _gelu_tanh(x):
    k = -2.0 * math.sqrt(2.0 / math.pi) * LOG2E
    return x / (1.0 + jnp.exp2(x * (k + (k * 0.044715) * (x * x))))


def _weight_rounding_specs(stacks, n_steps, step_of):
    in_specs, out_specs, out_shapes = [], [], []
    for w, layer in stacks:
        _, k, d = w.shape
        rows = k // n_steps
        assert rows * n_steps == k and rows % SUBLANES == 0
        in_specs.append(pl.BlockSpec((None, rows, d), lambda *g, layer=layer: (layer, step_of(*g), 0)))
        out_specs.append(pl.BlockSpec((rows, d), lambda *g: (step_of(*g), 0)))
        out_shapes.append(jax.ShapeDtypeStruct((k, d), BF16))
    return in_specs, out_specs, out_shapes


def _round_weight_slabs(w_refs, w_bf_refs):
    for w_ref, w_bf_ref in zip(w_refs, w_bf_refs):
        w_bf_ref[...] = w_ref[...].astype(BF16)


def _rmsnorm_kernel(x_ref, g_ref, o_ref):
    o_ref[...] = _rms(x_ref[...], g_ref[...]).astype(o_ref.dtype)


def rmsnorm(x, g, *, tm=512, out_dtype=BF16):
    m, d = x.shape
    return pl.pallas_call(
        _rmsnorm_kernel,
        grid=(m // tm,),
        in_specs=[pl.BlockSpec((tm, d), lambda i: (i, 0)),
                  pl.BlockSpec((1, d), lambda i: (0, 0))],
        out_specs=pl.BlockSpec((tm, d), lambda i: (i, 0)),
        out_shape=jax.ShapeDtypeStruct((m, d), out_dtype),
        compiler_params=_params("parallel"),
        name="rmsnorm",
    )(x, g.reshape(1, d))


def _matmul_kernel(x_ref, w_ref, o_ref, w_bf):
    @pl.when(pl.program_id(1) == 0)
    def _():
        w_bf[...] = w_ref[...].astype(BF16)

    o_ref[...] = jnp.dot(x_ref[...], w_bf[...], preferred_element_type=F32).astype(o_ref.dtype)


def matmul(x, w_stack, layer, *, tm=2048, tn=1024, out_dtype=BF16):
    m, k = x.shape
    n = w_stack.shape[2]
    return pl.pallas_call(
        _matmul_kernel,
        grid=(n // tn, m // tm),
        in_specs=[pl.BlockSpec((tm, k), lambda j, i: (i, 0)),
                  pl.BlockSpec((None, k, tn), lambda j, i: (layer, 0, j))],
        out_specs=pl.BlockSpec((tm, tn), lambda j, i: (i, j)),
        out_shape=jax.ShapeDtypeStruct((m, n), out_dtype),
        scratch_shapes=[pltpu.VMEM((k, tn), BF16)],
        compiler_params=_params("parallel", "arbitrary"),
        name="in_proj",
    )(x, w_stack)


def _proj_res_norm_kernel(*refs, n_pieces, emit_h):
    x_refs = refs[:n_pieces]
    w_ref, h_ref, g_ref = refs[n_pieces:n_pieces + 3]
    out_refs = refs[n_pieces + 3:]
    if n_pieces > 1:
        x = jnp.concatenate([r[...] for r in x_refs], axis=-1)
    else:
        x = x_refs[0][...]
    h_new = h_ref[...] + jnp.dot(x, w_ref[...], preferred_element_type=F32)
    if emit_h:
        out_refs[0][...] = h_new
    out_refs[-1][...] = _rms(h_new, g_ref[...]).astype(out_refs[-1].dtype)


def proj_res_norm(pieces, w, h, g, *, tm, norm_dtype=BF16, emit_h=True):
    m, d = h.shape
    k = w.shape[0]
    assert sum(p.shape[1] for p in pieces) == k
    in_specs = [pl.BlockSpec((tm, p.shape[1]), lambda i: (i, 0)) for p in pieces]
    in_specs += [pl.BlockSpec((k, d), lambda i: (0, 0), pipeline_mode=pl.Buffered(1)),
                 pl.BlockSpec((tm, d), lambda i: (i, 0)),
                 pl.BlockSpec((1, d), lambda i: (0, 0))]
    row_spec = pl.BlockSpec((tm, d), lambda i: (i, 0))
    norm_shape = jax.ShapeDtypeStruct((m, d), norm_dtype)
    if emit_h:
        out_specs, out_shape = [row_spec, row_spec], [jax.ShapeDtypeStruct((m, d), F32), norm_shape]
    else:
        out_specs, out_shape = row_spec, norm_shape
    return pl.pallas_call(
        functools.partial(_proj_res_norm_kernel, n_pieces=len(pieces), emit_h=emit_h),
        grid=(m // tm,),
        in_specs=in_specs,
        out_specs=out_specs,
        out_shape=out_shape,
        compiler_params=_params("parallel"),
        name="proj_res_norm",
    )(*pieces, w, h, g.reshape(1, d))


def _shift_rows(a, prev, shift):
    rolled = pltpu.roll(a, shift, 0)
    row = lax.broadcasted_iota(jnp.int32, (SUBLANES, a.shape[1]), 0)
    head = rolled[:SUBLANES]
    for r in range(shift):
        head = jnp.where(row == r, prev[SUBLANES - shift + r:SUBLANES - shift + r + 1], head)
    return jnp.concatenate([head, rolled[SUBLANES:]], axis=0)


def _up_conv_gate_kernel(x_ref, wg_ref, wv_ref, cwg_ref, cwv_ref, cbg_ref, cbv_ref, o_ref,
                         wg_bf, wv_bf, carry, *, tiles_per_seq):
    i = pl.program_id(1)

    @pl.when(i == 0)
    def _():
        wg_bf[...] = wg_ref[...].astype(BF16)
        wv_bf[...] = wv_ref[...].astype(BF16)

    @pl.when(i % tiles_per_seq == 0)
    def _():
        carry[...] = jnp.zeros_like(carry)

    x = x_ref[...]
    tm = x_ref.shape[0]

    a_g = jnp.dot(x, wg_bf[...], preferred_element_type=F32)
    prev_g = carry[0]
    carry[0] = a_g[tm - SUBLANES:]
    cwg = cwg_ref[...]
    sg = _silu(a_g * cwg[2:3] + _shift_rows(a_g, prev_g, 1) * cwg[1:2] + _shift_rows(a_g, prev_g, 2) * cwg[0:1]
               + cbg_ref[...])
    cwv = cwv_ref[...]
    g2, g1, g0, gb = sg * cwv[2:3], sg * cwv[1:2], sg * cwv[0:1], sg * cbv_ref[...]

    a_v = jnp.dot(x, wv_bf[...], preferred_element_type=F32)
    prev_v = carry[1]
    carry[1] = a_v[tm - SUBLANES:]
    o_ref[...] = (a_v * g2 + _shift_rows(a_v, prev_v, 1) * g1 + _shift_rows(a_v, prev_v, 2) * g0
                  + gb).astype(o_ref.dtype)


def up_conv_gate(x, w_up_stack, layer, conv_w, conv_b, *, seq, tm=1024, tn=512):
    m, k = x.shape
    d_ff = w_up_stack.shape[2] // 2
    nj = d_ff // tn
    conv_b = conv_b.reshape(1, 2 * d_ff)
    return pl.pallas_call(
        functools.partial(_up_conv_gate_kernel, tiles_per_seq=seq // tm),
        grid=(nj, m // tm),
        in_specs=[pl.BlockSpec((tm, k), lambda j, i: (i, 0)),
                  pl.BlockSpec((None, k, tn), lambda j, i: (layer, 0, j)),
                  pl.BlockSpec((None, k, tn), lambda j, i: (layer, 0, nj + j)),
                  pl.BlockSpec((CONV_WIDTH, tn), lambda j, i: (0, j)),
                  pl.BlockSpec((CONV_WIDTH, tn), lambda j, i: (0, nj + j)),
                  pl.BlockSpec((1, tn), lambda j, i: (0, j)),
                  pl.BlockSpec((1, tn), lambda j, i: (0, nj + j))],
        out_specs=pl.BlockSpec((tm, tn), lambda j, i: (i, j)),
        out_shape=jax.ShapeDtypeStruct((m, d_ff), BF16),
        scratch_shapes=[pltpu.VMEM((k, tn), BF16), pltpu.VMEM((k, tn), BF16),
                        pltpu.VMEM((2, SUBLANES, tn), F32)],
        compiler_params=_params("parallel", "arbitrary"),
        name="up_conv_gate",
    )(x, w_up_stack, w_up_stack, conv_w, conv_w, conv_b, conv_b)


def _pool_retention_kernel(cd_ref, z_ref, cos_ref, sin_ref, intra_ref, qdec_ref, kdec_ref, poolw_ref,
                           pscale_ref, gn_ref, *rest, tile, n_weights):
    w_refs, (a_ref, b_ref), w_bf_refs = rest[:n_weights], rest[n_weights:n_weights + 2], rest[n_weights + 2:-2]
    u_carry, state = rest[-2:]
    _round_weight_slabs(w_refs, w_bf_refs)
    t = pl.program_id(1)

    @pl.when(t == 0)
    def _():
        u_carry[...] = jnp.zeros_like(u_carry)
        state[...] = jnp.zeros_like(state)

    u = z_ref[:, :POOL_WIDTH].astype(F32)
    ext = jnp.concatenate([u_carry[...], u], axis=0)
    u_carry[...] = u[tile - POOL_HALO:]
    pos = (t * tile + 1 + lax.broadcasted_iota(jnp.int32, (tile, POOL_GROUP), 0)).astype(F32)
    for gi, w in enumerate(POOL_WINDOWS):
        cols = slice(gi * POOL_GROUP, (gi + 1) * POOL_GROUP)
        s = ext[:, cols]
        span = 1
        while span < w:
            s = s + pltpu.roll(s, span, 0)
            span *= 2
        pooled = s[POOL_HALO:] / jnp.minimum(pos, float(w))
        y = (pooled - u[:, cols]).astype(BF16)
        a = jnp.dot(y, poolw_ref[gi], preferred_element_type=F32)
        a_ref[:, cols] = (a * pscale_ref[:, cols]).astype(a_ref.dtype)

    q0 = POOL_WIDTH
    k0 = q0 + RET_HEADS * RET_QK_DIM
    v0 = k0 + RET_HEADS * RET_QK_DIM
    g0 = v0 + RET_HEADS * RET_V_DIM
    half = RET_QK_DIM // 2

    def chunk_body(c, carry):
        r0 = pl.multiple_of(c * RET_CHUNK, RET_CHUNK)
        rows = pl.ds(r0, RET_CHUNK)
        cos = cos_ref[rows, :]
        sin = sin_ref[rows, :]
        for h in range(RET_HEADS):
            q = z_ref[rows, q0 + h * RET_QK_DIM:q0 + (h + 1) * RET_QK_DIM].astype(F32)
            k = z_ref[rows, k0 + h * RET_QK_DIM:k0 + (h + 1) * RET_QK_DIM].astype(F32)
            v = z_ref[rows, v0 + h * RET_V_DIM:v0 + (h + 1) * RET_V_DIM]
            gate = z_ref[rows, g0 + h * RET_V_DIM:g0 + (h + 1) * RET_V_DIM].astype(F32)
            qr = q * cos + pltpu.roll(q, half, 1) * sin
            kr = (k * cos + pltpu.roll(k, half, 1) * sin) * (RET_QK_DIM ** -0.5)
            scores = lax.dot_general(qr.astype(BF16), kr.astype(BF16), (((1,), (1,)), ((), ())),
                                     preferred_element_type=F32) * intra_ref[h]
            inner = jnp.dot(scores.astype(BF16), v, preferred_element_type=F32)
            st = state[h]
            cross = jnp.dot((qr * qdec_ref[h]).astype(BF16), st.astype(BF16), preferred_element_type=F32)
            kv = lax.dot_general((kr * kdec_ref[h]).astype(BF16), v, (((0,), (0,)), ((), ())),
                                 preferred_element_type=F32)
            state[h] = st * cd_ref[h] + kv
            vcols = slice(h * RET_V_DIM, (h + 1) * RET_V_DIM)
            r = _layer_norm(inner + cross, gn_ref[:, vcols])
            b_ref[rows, vcols] = (r * _silu(gate)).astype(b_ref.dtype)
        return carry

    lax.fori_loop(0, tile // RET_CHUNK, chunk_body, 0)


def _retention_tables(seq):
    half = RET_QK_DIM // 2
    inv = 1.0 / (10000.0 ** (jnp.arange(half, dtype=F32) / half))
    ang = jnp.arange(seq, dtype=F32)[:, None] * inv[None, :]
    cos, sin = jnp.cos(ang), jnp.sin(ang)
    cos_full = jnp.concatenate([cos, cos], axis=-1)
    sin_signed = jnp.concatenate([-sin, sin], axis=-1)
    c = RET_CHUNK
    log_g = jnp.log(1.0 - 2.0 ** (-5.0 - jnp.arange(RET_HEADS, dtype=F32)))
    idx = jnp.arange(c, dtype=F32)
    diff = idx[:, None] - idx[None, :]
    intra = jnp.where(diff >= 0, jnp.exp(log_g[:, None, None] * jnp.maximum(diff, 0.0)), 0.0)
    q_dec = jnp.exp(log_g[:, None] * (idx[None, :] + 1.0))
    k_dec = jnp.exp(log_g[:, None] * (c - 1.0 - idx[None, :]))
    chunk_dec = jnp.exp(log_g * c)
    q_dec = jnp.broadcast_to(q_dec[:, :, None], (RET_HEADS, c, RET_QK_DIM))
    k_dec = jnp.broadcast_to(k_dec[:, :, None], (RET_HEADS, c, RET_QK_DIM))
    return cos_full, sin_signed, intra, q_dec, k_dec, chunk_dec


def pool_retention(z, pool_w, pool_scale, ret_gn_g, round_weights, *, batch, seq, tile=512):
    m, width = z.shape
    nt = seq // tile
    w_in_specs, w_out_specs, w_out_shapes = _weight_rounding_specs(round_weights, batch * nt, lambda b, t: b * nt + t)
    cos, sin, intra, q_dec, k_dec, chunk_dec = _retention_tables(seq)
    vw = RET_HEADS * RET_V_DIM
    const3 = lambda b, t: (0, 0, 0)
    return pl.pallas_call(
        functools.partial(_pool_retention_kernel, tile=tile, n_weights=len(round_weights)),
        grid=(batch, nt),
        in_specs=[pl.BlockSpec(memory_space=pltpu.SMEM),
                  pl.BlockSpec((tile, width), lambda b, t: (b * nt + t, 0)),
                  pl.BlockSpec((tile, RET_QK_DIM), lambda b, t: (t, 0)),
                  pl.BlockSpec((tile, RET_QK_DIM), lambda b, t: (t, 0)),
                  pl.BlockSpec(intra.shape, const3),
                  pl.BlockSpec(q_dec.shape, const3),
                  pl.BlockSpec(k_dec.shape, const3),
                  pl.BlockSpec(pool_w.shape, const3),
                  pl.BlockSpec((1, POOL_WIDTH), lambda b, t: (0, 0)),
                  pl.BlockSpec((1, vw), lambda b, t: (0, 0))] + w_in_specs,
        out_specs=[pl.BlockSpec((tile, POOL_WIDTH), lambda b, t: (b * nt + t, 0)),
                   pl.BlockSpec((tile, vw), lambda b, t: (b * nt + t, 0))] + w_out_specs,
        out_shape=[jax.ShapeDtypeStruct((m, POOL_WIDTH), BF16), jax.ShapeDtypeStruct((m, vw), BF16)] + w_out_shapes,
        scratch_shapes=[pltpu.VMEM((POOL_HALO, POOL_WIDTH), F32),
                        pltpu.VMEM((RET_HEADS, RET_QK_DIM, RET_V_DIM), F32)],
        compiler_params=_params("parallel", "arbitrary"),
        name="pool_retention",
    )(chunk_dec, z, cos, sin, intra, q_dec, k_dec, pool_w.astype(BF16),
      pool_scale.reshape(1, POOL_WIDTH), ret_gn_g.reshape(1, vw), *[w for w, _ in round_weights])


def _sgu_kernel(z_ref, lng_ref, ws_ref, bs_ref, o_ref, *, tile):
    zc = _gelu_tanh(z_ref[...].astype(F32))
    zu = zc[:, :SGU_WIDTH]
    v = _layer_norm(zc[:, SGU_WIDTH:], lng_ref[...]).astype(BF16)
    row = lax.broadcasted_iota(jnp.int32, (SGU_CHUNK, SGU_CHUNK), 0)
    col = lax.broadcasted_iota(jnp.int32, (SGU_CHUNK, SGU_CHUNK), 1)
    for g in range(SGU_GROUPS):
        wm = jnp.where(row >= col, ws_ref[g], 0.0).astype(BF16)
        bias = bs_ref[g]
        cols = slice(g * SGU_GROUP_DIM, (g + 1) * SGU_GROUP_DIM)
        for c in range(tile // SGU_CHUNK):
            rows = slice(c * SGU_CHUNK, (c + 1) * SGU_CHUNK)
            sv = jnp.dot(wm, v[rows, cols], preferred_element_type=F32) + bias
            o_ref[rows, cols] = (zu[rows, cols] * sv).astype(o_ref.dtype)


def spatial_gating(z, ln_g, w_s, b_s, *, tile=512):
    m = z.shape[0]
    return pl.pallas_call(
        functools.partial(_sgu_kernel, tile=tile),
        grid=(m // tile,),
        in_specs=[pl.BlockSpec((tile, 2 * SGU_WIDTH), lambda i: (i, 0)),
                  pl.BlockSpec((1, SGU_WIDTH), lambda i: (0, 0)),
                  pl.BlockSpec(w_s.shape, lambda i: (0, 0, 0)),
                  pl.BlockSpec((SGU_GROUPS, SGU_CHUNK, 1), lambda i: (0, 0, 0))],
        out_specs=pl.BlockSpec((tile, SGU_WIDTH), lambda i: (i, 0)),
        out_shape=jax.ShapeDtypeStruct((m, SGU_WIDTH), BF16),
        compiler_params=_params("parallel"),
        name="spatial_gating",
    )(z, ln_g.reshape(1, SGU_WIDTH), w_s, b_s.reshape(SGU_GROUPS, SGU_CHUNK, 1))


def _t5_bucket(rel):
    n = jnp.maximum(rel, 0)
    max_exact = N_BUCKETS // 2
    large = max_exact + (jnp.log(jnp.maximum(n, 1).astype(F32) / max_exact)
                         / math.log(MAX_DISTANCE / max_exact) * (N_BUCKETS - max_exact)).astype(jnp.int32)
    large = jnp.minimum(large, N_BUCKETS - 1)
    return jnp.where(n < max_exact, n, large)


def _bias_tiles_kernel(relb_ref, bucket_ref, o_ref):
    h = pl.program_id(0)
    bucket = bucket_ref[...]
    acc = jnp.zeros(bucket.shape, F32)
    for b in range(N_BUCKETS):
        acc = jnp.where(bucket == b, relb_ref[b, h], acc)
    far_bias = relb_ref[N_BUCKETS - 1, h]
    o_ref[0] = (acc - far_bias) * LOG2E


def rel_bias_tiles(rel_bias, *, tile):
    qk = jnp.arange(tile)[:, None] - jnp.arange(tile)[None, :]
    bucket = jnp.stack([_t5_bucket(qk), _t5_bucket(tile + qk)])
    return pl.pallas_call(
        _bias_tiles_kernel,
        grid=(DIFF_HEADS,),
        in_specs=[pl.BlockSpec(memory_space=pltpu.SMEM),
                  pl.BlockSpec((2, tile, tile), lambda h: (0, 0, 0))],
        out_specs=pl.BlockSpec((1, 2, tile, tile), lambda h: (h, 0, 0, 0)),
        out_shape=jax.ShapeDtypeStruct((DIFF_HEADS, 2, tile, tile), F32),
        compiler_params=_params("parallel"),
        name="rel_bias_tiles",
    )(rel_bias, bucket)


def _diff_attn_kernel(lq1_ref, lk1_ref, lq2_ref, lk2_ref, q_ref, k_ref, v_ref, bias_ref, subg_ref,
                      *rest, tile, seq, lam_init, n_weights):
    w_refs, o_ref, w_bf_refs = rest[:n_weights], rest[n_weights], rest[n_weights + 1:]
    _round_weight_slabs(w_refs, w_bf_refs)
    dh = DIFF_HEAD_DIM
    lam = (jnp.exp(jnp.sum(lq1_ref[...] * lk1_ref[...], axis=-1, keepdims=True))
           - jnp.exp(jnp.sum(lq2_ref[...] * lk2_ref[...], axis=-1, keepdims=True)) + lam_init)
    row = lax.broadcasted_iota(jnp.int32, (tile, tile), 0)
    col = lax.broadcasted_iota(jnp.int32, (tile, tile), 1)
    causal = row >= col

    def query_tile(qi):
        kv_len = (qi + 1) * tile
        rows = slice(qi * tile, kv_len)
        q = (q_ref[rows, :].astype(F32) * (dh ** -0.5 * LOG2E)).astype(BF16)
        streams = []
        for s in range(2):
            sc = lax.dot_general(q[:, s * dh:(s + 1) * dh], k_ref[:kv_len, s * dh:(s + 1) * dh],
                                 (((1,), (1,)), ((), ())), preferred_element_type=F32)
            parts = []
            if qi >= 2:
                parts.append(sc[:, :kv_len - 2 * tile])
            if qi >= 1:
                parts.append(sc[:, kv_len - 2 * tile:kv_len - tile] + bias_ref[0, 1])
            parts.append(jnp.where(causal, sc[:, kv_len - tile:] + bias_ref[0, 0], NEG_INF))
            sc = jnp.concatenate(parts, axis=1) if len(parts) > 1 else parts[0]
            p = jnp.exp2(sc - jnp.max(sc, axis=-1, keepdims=True))
            l = jnp.sum(p, axis=-1, keepdims=True)
            streams.append(jnp.dot(p.astype(BF16), v_ref[:kv_len, :], preferred_element_type=F32) / l)
        d = streams[0] - lam * streams[1]
        o_ref[rows, :] = (_rms(d, subg_ref[...]) * (1.0 - lam_init)).astype(o_ref.dtype)

    n_tiles = seq // tile
    always = pl.program_id(0) >= 0
    for first in range(n_tiles // 2):
        @pl.when(always)
        def _(first=first):
            query_tile(first)
            query_tile(n_tiles - 1 - first)


def diff_attention(z, bias_tiles, lq1, lk1, lq2, lk2, subln_g, round_weights, *, batch, seq, layer_idx,
                   tile):
    m = z.shape[0]
    assert tile >= MAX_DISTANCE and seq % (2 * tile) == 0
    w_in_specs, w_out_specs, w_out_shapes = _weight_rounding_specs(
        round_weights, batch * DIFF_HEADS, lambda b, h: b * DIFF_HEADS + h)
    w = 2 * DIFF_HEAD_DIM
    q_blk = 2 * SGU_WIDTH // w
    k_blk = q_blk + DIFF_HEADS
    v_blk = k_blk + DIFF_HEADS
    lam_init = 0.8 - 0.6 * math.exp(-0.3 * layer_idx)
    vec = lambda a: a.reshape(1, DIFF_HEAD_DIM)
    vec_spec = pl.BlockSpec((1, DIFF_HEAD_DIM), lambda b, h: (0, 0))
    return pl.pallas_call(
        functools.partial(_diff_attn_kernel, tile=tile, seq=seq, lam_init=lam_init, n_weights=len(round_weights)),
        grid=(batch, DIFF_HEADS),
        in_specs=[vec_spec, vec_spec, vec_spec, vec_spec,
                  pl.BlockSpec((seq, w), lambda b, h: (b, q_blk + h)),
                  pl.BlockSpec((seq, w), lambda b, h: (b, k_blk + h)),
                  pl.BlockSpec((seq, w), lambda b, h: (b, v_blk + h)),
                  pl.BlockSpec((1, 2, tile, tile), lambda b, h: (h, 0, 0, 0)),
                  pl.BlockSpec((1, DIFF_V_DIM), lambda b, h: (0, 0))] + w_in_specs,
        out_specs=[pl.BlockSpec((seq, DIFF_V_DIM), lambda b, h: (b, h))] + w_out_specs,
        out_shape=[jax.ShapeDtypeStruct((m, DIFF_HEADS * DIFF_V_DIM), BF16)] + w_out_shapes,
        compiler_params=_params("parallel", "parallel"),
        name="diff_attention",
    )(vec(lq1), vec(lk1), vec(lq2), vec(lk2), z, z, z, bias_tiles, subln_g.reshape(1, DIFF_V_DIM),
      *[w for w, _ in round_weights])


ATTN_TILE = 256
ROW_TILE_OUT = 512
ROW_TILE_DOWN = 256


def kernel(x, w_in_even, w_out_even, pool_w, pool_scale, ret_gn_g, w_in_odd, w_out_odd, sgu_ln_g, sgu_w, sgu_b,
           lam_q1, lam_k1, lam_q2, lam_k2, diff_subln_g, rel_bias, mix_norm_g, ffn_norm_g, w_up, conv_w, conv_b,
           w_down, final_norm_g):
    batch, seq, d = x.shape
    depth = mix_norm_g.shape[0]
    h = x.reshape(batch * seq, d)
    hn = rmsnorm(h, mix_norm_g[0])
    bias_tiles = rel_bias_tiles(rel_bias, tile=ATTN_TILE)
    out = None
    for i in range(depth):
        if i % 2 == 0:
            e = i // 2
            z = matmul(hn, w_in_even, e)
            a_out, b_out, w_out, w_dn = pool_retention(z, pool_w[e], pool_scale[e], ret_gn_g[e],
                                                        [(w_out_even, e), (w_down, i)], batch=batch, seq=seq)
            pieces = [a_out, b_out]
        else:
            o = i // 2
            z = matmul(hn, w_in_odd, o)
            c_out = spatial_gating(z, sgu_ln_g[o], sgu_w[o], sgu_b[o])
            d_out, w_out, w_dn = diff_attention(z, bias_tiles, lam_q1[o], lam_k1[o], lam_q2[o], lam_k2[o],
                                                diff_subln_g[o], [(w_out_odd, o), (w_down, i)],
                                                batch=batch, seq=seq, layer_idx=i, tile=ATTN_TILE)
            pieces = [c_out, d_out]
        h, hn = proj_res_norm(pieces, w_out, h, ffn_norm_g[i], tm=ROW_TILE_OUT)
        act = up_conv_gate(hn, w_up, i, conv_w[i], conv_b[i], seq=seq)
        if i + 1 < depth:
            h, hn = proj_res_norm([act], w_dn, h, mix_norm_g[i + 1], tm=ROW_TILE_DOWN)
        else:
            out = proj_res_norm([act], w_dn, h, final_norm_g, tm=ROW_TILE_DOWN,
                                norm_dtype=x.dtype, emit_h=False)
    return out.reshape(batch, seq, d)
```

```python
import functools
import math

import jax
import jax.numpy as jnp
import numpy as np
from jax import lax
from jax.experimental import pallas as pl
from jax.experimental.pallas import tpu as pltpu

EPS = 1e-6
POOL_WINDOWS = (2, 4, 8, 16)
POOL_GROUP = 128
POOL_WIDTH = len(POOL_WINDOWS) * POOL_GROUP
POOL_HALO = 16
RET_HEADS = 6
RET_QK_DIM = 128
RET_V_DIM = 256
RET_CHUNK = 128
SGU_CHUNK = 128
SGU_GROUPS = 8
SGU_GROUP_DIM = 128
SGU_WIDTH = SGU_GROUPS * SGU_GROUP_DIM
DIFF_HEADS = 4
DIFF_HEAD_DIM = 128
DIFF_V_DIM = 2 * DIFF_HEAD_DIM
N_BUCKETS = 32
MAX_DISTANCE = 128
CONV_WIDTH = 3
NEG_INF = -1e30
LOG2E = 1.4426950408889634

V7X_VMEM_BYTES = 64 * 1024 * 1024
VMEM_LIMIT = V7X_VMEM_BYTES - 8 * 1024 * 1024
SUBLANES = 8

F32 = jnp.float32
BF16 = jnp.bfloat16


def _params(*semantics):
    return pltpu.CompilerParams(dimension_semantics=semantics, vmem_limit_bytes=VMEM_LIMIT)


def _rms(x, g):
    return x * lax.rsqrt(jnp.mean(x * x, axis=-1, keepdims=True) + EPS) * g


def _layer_norm(x, g):
    mu = jnp.mean(x, axis=-1, keepdims=True)
    xc = x - mu
    var = jnp.mean(xc * xc, axis=-1, keepdims=True)
    return xc * lax.rsqrt(var + EPS) * g


def _silu(x):
    return x * (1.0 / (1.0 + jnp.exp(-x)))


def _gelu_tanh(x):
    k = -2.0 * math.sqrt(2.0 / math.pi) * LOG2E
    return x / (1.0 + jnp.exp2(x * (k + (k * 0.044715) * (x * x))))


def _weight_rounding_specs(stacks, n_steps, step_of):
    in_specs, out_specs, out_shapes = [], [], []
    for w, layer in stacks:
        _, k, d = w.shape
        rows = k // n_steps
        assert rows * n_steps == k and rows % SUBLANES == 0
        in_specs.append(pl.BlockSpec((None, rows, d), lambda *g, layer=layer: (layer, step_of(*g), 0)))
        out_specs.append(pl.BlockSpec((rows, d), lambda *g: (step_of(*g), 0)))
        out_shapes.append(jax.ShapeDtypeStruct((k, d), BF16))
    return in_specs, out_specs, out_shapes


def _round_weight_slabs(w_refs, w_bf_refs):
    for w_ref, w_bf_ref in zip(w_refs, w_bf_refs):
        w_bf_ref[...] = w_ref[...].astype(BF16)


def _rmsnorm_kernel(x_ref, g_ref, o_ref):
    o_ref[...] = _rms(x_ref[...], g_ref[...]).astype(o_ref.dtype)


def rmsnorm(x, g, *, tm=512, out_dtype=BF16):
    m, d = x.shape
    return pl.pallas_call(
        _rmsnorm_kernel,
        grid=(m // tm,),
        in_specs=[pl.BlockSpec((tm, d), lambda i: (i, 0)),
                  pl.BlockSpec((1, d), lambda i: (0, 0))],
        out_specs=pl.BlockSpec((tm, d), lambda i: (i, 0)),
        out_shape=jax.ShapeDtypeStruct((m, d), out_dtype),
        compiler_params=_params("parallel"),
        name="rmsnorm",
    )(x, g.reshape(1, d))


def _matmul_kernel(x_ref, w_ref, o_ref, w_bf):
    @pl.when(pl.program_id(1) == 0)
    def _():
        w_bf[...] = w_ref[...].astype(BF16)

    o_ref[...] = jnp.dot(x_ref[...], w_bf[...], preferred_element_type=F32).astype(o_ref.dtype)


def matmul(x, w_stack, layer, *, tm=2048, tn=1024, out_dtype=BF16):
    m, k = x.shape
    n = w_stack.shape[2]
    return pl.pallas_call(
        _matmul_kernel,
        grid=(n // tn, m // tm),
        in_specs=[pl.BlockSpec((tm, k), lambda j, i: (i, 0)),
                  pl.BlockSpec((None, k, tn), lambda j, i: (layer, 0, j))],
        out_specs=pl.BlockSpec((tm, tn), lambda j, i: (i, j)),
        out_shape=jax.ShapeDtypeStruct((m, n), out_dtype),
        scratch_shapes=[pltpu.VMEM((k, tn), BF16)],
        compiler_params=_params("parallel", "arbitrary"),
        name="in_proj",
    )(x, w_stack)


def _proj_res_norm_kernel(*refs, n_pieces, emit_h):
    x_refs = refs[:n_pieces]
    w_ref, h_ref, g_ref = refs[n_pieces:n_pieces + 3]
    out_refs = refs[n_pieces + 3:]
    if n_pieces > 1:
        x = jnp.concatenate([r[...] for r in x_refs], axis=-1)
    else:
        x = x_refs[0][...]
    h_new = h_ref[...] + jnp.dot(x, w_ref[...], preferred_element_type=F32)
    if emit_h:
        out_refs[0][...] = h_new
    out_refs[-1][...] = _rms(h_new, g_ref[...]).astype(out_refs[-1].dtype)


def proj_res_norm(pieces, w, h, g, *, tm, norm_dtype=BF16, emit_h=True):
    m, d = h.shape
    k = w.shape[0]
    assert sum(p.shape[1] for p in pieces) == k
    in_specs = [pl.BlockSpec((tm, p.shape[1]), lambda i: (i, 0)) for p in pieces]
    in_specs += [pl.BlockSpec((k, d), lambda i: (0, 0), pipeline_mode=pl.Buffered(1)),
                 pl.BlockSpec((tm, d), lambda i: (i, 0)),
                 pl.BlockSpec((1, d), lambda i: (0, 0))]
    row_spec = pl.BlockSpec((tm, d), lambda i: (i, 0))
    norm_shape = jax.ShapeDtypeStruct((m, d), norm_dtype)
    if emit_h:
        out_specs, out_shape = [row_spec, row_spec], [jax.ShapeDtypeStruct((m, d), F32), norm_shape]
    else:
        out_specs, out_shape = row_spec, norm_shape
    return pl.pallas_call(
        functools.partial(_proj_res_norm_kernel, n_pieces=len(pieces), emit_h=emit_h),
        grid=(m // tm,),
        in_specs=in_specs,
        out_specs=out_specs,
        out_shape=out_shape,
        compiler_params=_params("parallel"),
        name="proj_res_norm",
    )(*pieces, w, h, g.reshape(1, d))


def _shift_rows(a, prev, shift):
    rolled = pltpu.roll(a, shift, 0)
    row = lax.broadcasted_iota(jnp.int32, (SUBLANES, a.shape[1]), 0)
    head = rolled[:SUBLANES]
    for r in range(shift):
        head = jnp.where(row == r, prev[SUBLANES - shift + r:SUBLANES - shift + r + 1], head)
    return jnp.concatenate([head, rolled[SUBLANES:]], axis=0)


def _up_conv_gate_kernel(x_ref, wg_ref, wv_ref, cwg_ref, cwv_ref, cbg_ref, cbv_ref, o_ref,
                         wg_bf, wv_bf, carry, *, tiles_per_seq):
    i = pl.program_id(1)

    @pl.when(i == 0)
    def _():
        wg_bf[...] = wg_ref[...].astype(BF16)
        wv_bf[...] = wv_ref[...].astype(BF16)

    @pl.when(i % tiles_per_seq == 0)
    def _():
        carry[...] = jnp.zeros_like(carry)

    x = x_ref[...]
    tm = x_ref.shape[0]

    a_g = jnp.dot(x, wg_bf[...], preferred_element_type=F32)
    prev_g = carry[0]
    carry[0] = a_g[tm - SUBLANES:]
    cwg = cwg_ref[...]
    sg = _silu(a_g * cwg[2:3] + _shift_rows(a_g, prev_g, 1) * cwg[1:2] + _shift_rows(a_g, prev_g, 2) * cwg[0:1]
               + cbg_ref[...])
    cwv = cwv_ref[...]
    g2, g1, g0, gb = sg * cwv[2:3], sg * cwv[1:2], sg * cwv[0:1], sg * cbv_ref[...]

    a_v = jnp.dot(x, wv_bf[...], preferred_element_type=F32)
    prev_v = carry[1]
    carry[1] = a_v[tm - SUBLANES:]
    o_ref[...] = (a_v * g2 + _shift_rows(a_v, prev_v, 1) * g1 + _shift_rows(a_v, prev_v, 2) * g0
                  + gb).astype(o_ref.dtype)


def up_conv_gate(x, w_up_stack, layer, conv_w, conv_b, *, seq, tm=1024, tn=512):
    m, k = x.shape
    d_ff = w_up_stack.shape[2] // 2
    nj = d_ff // tn
    conv_b = conv_b.reshape(1, 2 * d_ff)
    return pl.pallas_call(
        functools.partial(_up_conv_gate_kernel, tiles_per_seq=seq // tm),
        grid=(nj, m // tm),
        in_specs=[pl.BlockSpec((tm, k), lambda j, i: (i, 0)),
                  pl.BlockSpec((None, k, tn), lambda j, i: (layer, 0, j)),
                  pl.BlockSpec((None, k, tn), lambda j, i: (layer, 0, nj + j)),
                  pl.BlockSpec((CONV_WIDTH, tn), lambda j, i: (0, j)),
                  pl.BlockSpec((CONV_WIDTH, tn), lambda j, i: (0, nj + j)),
                  pl.BlockSpec((1, tn), lambda j, i: (0, j)),
                  pl.BlockSpec((1, tn), lambda j, i: (0, nj + j))],
        out_specs=pl.BlockSpec((tm, tn), lambda j, i: (i, j)),
        out_shape=jax.ShapeDtypeStruct((m, d_ff), BF16),
        scratch_shapes=[pltpu.VMEM((k, tn), BF16), pltpu.VMEM((k, tn), BF16),
                        pltpu.VMEM((2, SUBLANES, tn), F32)],
        compiler_params=_params("parallel", "arbitrary"),
        name="up_conv_gate",
    )(x, w_up_stack, w_up_stack, conv_w, conv_w, conv_b, conv_b)


def _pool_retention_kernel(cd_ref, z_ref, cos_ref, sin_ref, intra_ref, qdec_ref, kdec_ref, poolw_ref,
                           pscale_ref, gn_ref, *rest, tile, n_weights):
    w_refs, (a_ref, b_ref), w_bf_refs = rest[:n_weights], rest[n_weights:n_weights + 2], rest[n_weights + 2:-2]
    u_carry, state = rest[-2:]
    _round_weight_slabs(w_refs, w_bf_refs)
    t = pl.program_id(1)

    @pl.when(t == 0)
    def _():
        u_carry[...] = jnp.zeros_like(u_carry)
        state[...] = jnp.zeros_like(state)

    u = z_ref[:, :POOL_WIDTH].astype(F32)
    ext = jnp.concatenate([u_carry[...], u], axis=0)
    u_carry[...] = u[tile - POOL_HALO:]
    pos = (t * tile + 1 + lax.broadcasted_iota(jnp.int32, (tile, POOL_GROUP), 0)).astype(F32)
    for gi, w in enumerate(POOL_WINDOWS):
        cols = slice(gi * POOL_GROUP, (gi + 1) * POOL_GROUP)
        s = ext[:, cols]
        span = 1
        while span < w:
            s = s + pltpu.roll(s, span, 0)
            span *= 2
        pooled = s[POOL_HALO:] / jnp.minimum(pos, float(w))
        y = (pooled - u[:, cols]).astype(BF16)
        a = jnp.dot(y, poolw_ref[gi], preferred_element_type=F32)
        a_ref[:, cols] = (a * pscale_ref[:, cols]).astype(a_ref.dtype)

    q0 = POOL_WIDTH
    k0 = q0 + RET_HEADS * RET_QK_DIM
    v0 = k0 + RET_HEADS * RET_QK_DIM
    g0 = v0 + RET_HEADS * RET_V_DIM
    half = RET_QK_DIM // 2

    def chunk_body(c, carry):
        r0 = pl.multiple_of(c * RET_CHUNK, RET_CHUNK)
        rows = pl.ds(r0, RET_CHUNK)
        cos = cos_ref[rows, :]
        sin = sin_ref[rows, :]
        for h in range(RET_HEADS):
            q = z_ref[rows, q0 + h * RET_QK_DIM:q0 + (h + 1) * RET_QK_DIM].astype(F32)
            k = z_ref[rows, k0 + h * RET_QK_DIM:k0 + (h + 1) * RET_QK_DIM].astype(F32)
            v = z_ref[rows, v0 + h * RET_V_DIM:v0 + (h + 1) * RET_V_DIM]
            gate = z_ref[rows, g0 + h * RET_V_DIM:g0 + (h + 1) * RET_V_DIM].astype(F32)
            qr = q * cos + pltpu.roll(q, half, 1) * sin
            kr = (k * cos + pltpu.roll(k, half, 1) * sin) * (RET_QK_DIM ** -0.5)
            scores = lax.dot_general(qr.astype(BF16), kr.astype(BF16), (((1,), (1,)), ((), ())),
                                     preferred_element_type=F32) * intra_ref[h]
            inner = jnp.dot(scores.astype(BF16), v, preferred_element_type=F32)
            st = state[h]
            cross = jnp.dot((qr * qdec_ref[h]).astype(BF16), st.astype(BF16), preferred_element_type=F32)
            kv = lax.dot_general((kr * kdec_ref[h]).astype(BF16), v, (((0,), (0,)), ((), ())),
                                 preferred_element_type=F32)
            state[h] = st * cd_ref[h] + kv
            vcols = slice(h * RET_V_DIM, (h + 1) * RET_V_DIM)
            r = _layer_norm(inner + cross, gn_ref[:, vcols])
            b_ref[rows, vcols] = (r * _silu(gate)).astype(b_ref.dtype)
        return carry

    lax.fori_loop(0, tile // RET_CHUNK, chunk_body, 0)


def _retention_tables(seq):
    f32 = np.float32
    half = RET_QK_DIM // 2
    inv = (1.0 / (10000.0 ** (np.arange(half, dtype=f32) / f32(half)))).astype(f32)
    ang = np.arange(seq, dtype=f32)[:, None] * inv[None, :]
    cos, sin = np.cos(ang), np.sin(ang)
    cos_full = np.concatenate([cos, cos], axis=-1)
    sin_signed = np.concatenate([-sin, sin], axis=-1)
    c = RET_CHUNK
    log_g = np.log(1.0 - 2.0 ** (-5.0 - np.arange(RET_HEADS, dtype=f32))).astype(f32)
    idx = np.arange(c, dtype=f32)
    diff = idx[:, None] - idx[None, :]
    intra = np.where(diff >= 0, np.exp(log_g[:, None, None] * np.maximum(diff, 0.0)), 0.0).astype(f32)
    q_dec = np.exp(log_g[:, None] * (idx[None, :] + 1.0)).astype(f32)
    k_dec = np.exp(log_g[:, None] * (c - 1.0 - idx[None, :])).astype(f32)
    chunk_dec = np.exp(log_g * c).astype(f32)
    q_dec = np.ascontiguousarray(np.broadcast_to(q_dec[:, :, None], (RET_HEADS, c, RET_QK_DIM)))
    k_dec = np.ascontiguousarray(np.broadcast_to(k_dec[:, :, None], (RET_HEADS, c, RET_QK_DIM)))
    return cos_full, sin_signed, intra, q_dec, k_dec, chunk_dec


def pool_retention(z, pool_w, pool_scale, ret_gn_g, round_weights, *, batch, seq, tile=512):
    m, width = z.shape
    nt = seq // tile
    w_in_specs, w_out_specs, w_out_shapes = _weight_rounding_specs(round_weights, batch * nt, lambda b, t: b * nt + t)
    cos, sin, intra, q_dec, k_dec, chunk_dec = _retention_tables(seq)
    vw = RET_HEADS * RET_V_DIM
    const3 = lambda b, t: (0, 0, 0)
    return pl.pallas_call(
        functools.partial(_pool_retention_kernel, tile=tile, n_weights=len(round_weights)),
        grid=(batch, nt),
        in_specs=[pl.BlockSpec(memory_space=pltpu.SMEM),
                  pl.BlockSpec((tile, width), lambda b, t: (b * nt + t, 0)),
                  pl.BlockSpec((tile, RET_QK_DIM), lambda b, t: (t, 0)),
                  pl.BlockSpec((tile, RET_QK_DIM), lambda b, t: (t, 0)),
                  pl.BlockSpec(intra.shape, const3),
                  pl.BlockSpec(q_dec.shape, const3),
                  pl.BlockSpec(k_dec.shape, const3),
                  pl.BlockSpec(pool_w.shape, const3),
                  pl.BlockSpec((1, POOL_WIDTH), lambda b, t: (0, 0)),
                  pl.BlockSpec((1, vw), lambda b, t: (0, 0))] + w_in_specs,
        out_specs=[pl.BlockSpec((tile, POOL_WIDTH), lambda b, t: (b * nt + t, 0)),
                   pl.BlockSpec((tile, vw), lambda b, t: (b * nt + t, 0))] + w_out_specs,
        out_shape=[jax.ShapeDtypeStruct((m, POOL_WIDTH), BF16), jax.ShapeDtypeStruct((m, vw), BF16)] + w_out_shapes,
        scratch_shapes=[pltpu.VMEM((POOL_HALO, POOL_WIDTH), F32),
                        pltpu.VMEM((RET_HEADS, RET_QK_DIM, RET_V_DIM), F32)],
        compiler_params=_params("parallel", "arbitrary"),
        name="pool_retention",
    )(chunk_dec, z, cos, sin, intra, q_dec, k_dec, pool_w.astype(BF16),
      pool_scale.reshape(1, POOL_WIDTH), ret_gn_g.reshape(1, vw), *[w for w, _ in round_weights])


def _sgu_kernel(z_ref, lng_ref, ws_ref, bs_ref, o_ref, *, tile):
    zc = _gelu_tanh(z_ref[...].astype(F32))
    zu = zc[:, :SGU_WIDTH]
    v = _layer_norm(zc[:, SGU_WIDTH:], lng_ref[...]).astype(BF16)
    row = lax.broadcasted_iota(jnp.int32, (SGU_CHUNK, SGU_CHUNK), 0)
    col = lax.broadcasted_iota(jnp.int32, (SGU_CHUNK, SGU_CHUNK), 1)
    for g in range(SGU_GROUPS):
        wm = jnp.where(row >= col, ws_ref[g], 0.0).astype(BF16)
        bias = bs_ref[g]
        cols = slice(g * SGU_GROUP_DIM, (g + 1) * SGU_GROUP_DIM)
        for c in range(tile // SGU_CHUNK):
            rows = slice(c * SGU_CHUNK, (c + 1) * SGU_CHUNK)
            sv = jnp.dot(wm, v[rows, cols], preferred_element_type=F32) + bias
            o_ref[rows, cols] = (zu[rows, cols] * sv).astype(o_ref.dtype)


def spatial_gating(z, ln_g, w_s, b_s, *, tile=512):
    m = z.shape[0]
    return pl.pallas_call(
        functools.partial(_sgu_kernel, tile=tile),
        grid=(m // tile,),
        in_specs=[pl.BlockSpec((tile, 2 * SGU_WIDTH), lambda i: (i, 0)),
                  pl.BlockSpec((1, SGU_WIDTH), lambda i: (0, 0)),
                  pl.BlockSpec(w_s.shape, lambda i: (0, 0, 0)),
                  pl.BlockSpec((SGU_GROUPS, SGU_CHUNK, 1), lambda i: (0, 0, 0))],
        out_specs=pl.BlockSpec((tile, SGU_WIDTH), lambda i: (i, 0)),
        out_shape=jax.ShapeDtypeStruct((m, SGU_WIDTH), BF16),
        compiler_params=_params("parallel"),
        name="spatial_gating",
    )(z, ln_g.reshape(1, SGU_WIDTH), w_s, b_s.reshape(SGU_GROUPS, SGU_CHUNK, 1))


def _t5_bucket(rel):
    n = np.maximum(rel, 0)
    max_exact = N_BUCKETS // 2
    large = max_exact + (np.log(np.maximum(n, 1).astype(np.float32) / np.float32(max_exact))
                         / np.float32(math.log(MAX_DISTANCE / max_exact))
                         * np.float32(N_BUCKETS - max_exact)).astype(np.int32)
    large = np.minimum(large, N_BUCKETS - 1)
    return np.where(n < max_exact, n, large).astype(np.int32)


def _bias_tiles_kernel(relb_ref, bucket_ref, o_ref):
    h = pl.program_id(0)
    bucket = bucket_ref[...]
    acc = jnp.zeros(bucket.shape, F32)
    for b in range(N_BUCKETS):
        acc = jnp.where(bucket == b, relb_ref[b, h], acc)
    far_bias = relb_ref[N_BUCKETS - 1, h]
    o_ref[0] = (acc - far_bias) * LOG2E


def rel_bias_tiles(rel_bias, *, tile):
    qk = np.arange(tile)[:, None] - np.arange(tile)[None, :]
    bucket = np.stack([_t5_bucket(qk), _t5_bucket(tile + qk)])
    return pl.pallas_call(
        _bias_tiles_kernel,
        grid=(DIFF_HEADS,),
        in_specs=[pl.BlockSpec(memory_space=pltpu.SMEM),
                  pl.BlockSpec((2, tile, tile), lambda h: (0, 0, 0))],
        out_specs=pl.BlockSpec((1, 2, tile, tile), lambda h: (h, 0, 0, 0)),
        out_shape=jax.ShapeDtypeStruct((DIFF_HEADS, 2, tile, tile), F32),
        compiler_params=_params("parallel"),
        name="rel_bias_tiles",
    )(rel_bias, bucket)


def _diff_attn_kernel(lq1_ref, lk1_ref, lq2_ref, lk2_ref, q_ref, k_ref, v_ref, bias_ref, subg_ref,
                      *rest, tile, seq, lam_init, n_weights):
    w_refs, o_ref, w_bf_refs = rest[:n_weights], rest[n_weights], rest[n_weights + 1:]
    _round_weight_slabs(w_refs, w_bf_refs)
    dh = DIFF_HEAD_DIM
    lam = (jnp.exp(jnp.sum(lq1_ref[...] * lk1_ref[...], axis=-1, keepdims=True))
           - jnp.exp(jnp.sum(lq2_ref[...] * lk2_ref[...], axis=-1, keepdims=True)) + lam_init)
    row = lax.broadcasted_iota(jnp.int32, (tile, tile), 0)
    col = lax.broadcasted_iota(jnp.int32, (tile, tile), 1)
    causal = row >= col

    def query_tile(qi):
        kv_len = (qi + 1) * tile
        rows = slice(qi * tile, kv_len)
        q = (q_ref[rows, :].astype(F32) * (dh ** -0.5 * LOG2E)).astype(BF16)
        streams = []
        for s in range(2):
            sc = lax.dot_general(q[:, s * dh:(s + 1) * dh], k_ref[:kv_len, s * dh:(s + 1) * dh],
                                 (((1,), (1,)), ((), ())), preferred_element_type=F32)
            parts = []
            if qi >= 2:
                parts.append(sc[:, :kv_len - 2 * tile])
            if qi >= 1:
                parts.append(sc[:, kv_len - 2 * tile:kv_len - tile] + bias_ref[0, 1])
            parts.append(jnp.where(causal, sc[:, kv_len - tile:] + bias_ref[0, 0], NEG_INF))
            sc = jnp.concatenate(parts, axis=1) if len(parts) > 1 else parts[0]
            p = jnp.exp2(sc - jnp.max(sc, axis=-1, keepdims=True))
            l = jnp.sum(p, axis=-1, keepdims=True)
            streams.append(jnp.dot(p.astype(BF16), v_ref[:kv_len, :], preferred_element_type=F32) / l)
        d = streams[0] - lam * streams[1]
        o_ref[rows, :] = (_rms(d, subg_ref[...]) * (1.0 - lam_init)).astype(o_ref.dtype)

    n_tiles = seq // tile
    always = pl.program_id(0) >= 0
    for first in range(n_tiles // 2):
        @pl.when(always)
        def _(first=first):
            query_tile(first)
            query_tile(n_tiles - 1 - first)


def diff_attention(z, bias_tiles, lq1, lk1, lq2, lk2, subln_g, round_weights, *, batch, seq, layer_idx,
                   tile):
    m = z.shape[0]
    assert tile >= MAX_DISTANCE and seq % (2 * tile) == 0
    w_in_specs, w_out_specs, w_out_shapes = _weight_rounding_specs(
        round_weights, batch * DIFF_HEADS, lambda b, h: b * DIFF_HEADS + h)
    w = 2 * DIFF_HEAD_DIM
    q_blk = 2 * SGU_WIDTH // w
    k_blk = q_blk + DIFF_HEADS
    v_blk = k_blk + DIFF_HEADS
    lam_init = 0.8 - 0.6 * math.exp(-0.3 * layer_idx)
    vec = lambda a: a.reshape(1, DIFF_HEAD_DIM)
    vec_spec = pl.BlockSpec((1, DIFF_HEAD_DIM), lambda b, h: (0, 0))
    return pl.pallas_call(
        functools.partial(_diff_attn_kernel, tile=tile, seq=seq, lam_init=lam_init, n_weights=len(round_weights)),
        grid=(batch, DIFF_HEADS),
        in_specs=[vec_spec, vec_spec, vec_spec, vec_spec,
                  pl.BlockSpec((seq, w), lambda b, h: (b, q_blk + h)),
                  pl.BlockSpec((seq, w), lambda b, h: (b, k_blk + h)),
                  pl.BlockSpec((seq, w), lambda b, h: (b, v_blk + h)),
                  pl.BlockSpec((1, 2, tile, tile), lambda b, h: (h, 0, 0, 0)),
                  pl.BlockSpec((1, DIFF_V_DIM), lambda b, h: (0, 0))] + w_in_specs,
        out_specs=[pl.BlockSpec((seq, DIFF_V_DIM), lambda b, h: (b, h))] + w_out_specs,
        out_shape=[jax.ShapeDtypeStruct((m, DIFF_HEADS * DIFF_V_DIM), BF16)] + w_out_shapes,
        compiler_params=_params("parallel", "parallel"),
        name="diff_attention",
    )(vec(lq1), vec(lk1), vec(lq2), vec(lk2), z, z, z, bias_tiles, subln_g.reshape(1, DIFF_V_DIM),
      *[w for w, _ in round_weights])


ATTN_TILE = 256
ROW_TILE_OUT = 512
ROW_TILE_DOWN = 256


def kernel(x, w_in_even, w_out_even, pool_w, pool_scale, ret_gn_g, w_in_odd, w_out_odd, sgu_ln_g, sgu_w, sgu_b,
           lam_q1, lam_k1, lam_q2, lam_k2, diff_subln_g, rel_bias, mix_norm_g, ffn_norm_g, w_up, conv_w, conv_b,
           w_down, final_norm_g):
    batch, seq, d = x.shape
    depth = mix_norm_g.shape[0]
    h = x.reshape(batch * seq, d)
    hn = rmsnorm(h, mix_norm_g[0])
    bias_tiles = rel_bias_tiles(rel_bias, tile=ATTN_TILE)
    out = None
    for i in range(depth):
        if i % 2 == 0:
            e = i // 2
            z = matmul(hn, w_in_even, e)
            a_out, b_out, w_out, w_dn = pool_retention(z, pool_w[e], pool_scale[e], ret_gn_g[e],
                                                        [(w_out_even, e), (w_down, i)], batch=batch, seq=seq)
            pieces = [a_out, b_out]
        else:
            o = i // 2
            z = matmul(hn, w_in_odd, o)
            c_out = spatial_gating(z, sgu_ln_g[o], sgu_w[o], sgu_b[o])
            d_out, w_out, w_dn = diff_attention(z, bias_tiles, lam_q1[o], lam_k1[o], lam_q2[o], lam_k2[o],
                                                diff_subln_g[o], [(w_out_odd, o), (w_down, i)],
                                                batch=batch, seq=seq, layer_idx=i, tile=ATTN_TILE)
            pieces = [c_out, d_out]
        h, hn = proj_res_norm(pieces, w_out, h, ffn_norm_g[i], tm=ROW_TILE_OUT)
        act = up_conv_gate(hn, w_up, i, conv_w[i], conv_b[i], seq=seq)
        if i + 1 < depth:
            h, hn = proj_res_norm([act], w_dn, h, mix_norm_g[i + 1], tm=ROW_TILE_DOWN)
        else:
            out = proj_res_norm([act], w_dn, h, final_norm_g, tm=ROW_TILE_DOWN,
                                norm_dtype=x.dtype, emit_h=False)
    return out.reshape(batch, seq, d)
```

```python
import functools
import math

import jax
import jax.numpy as jnp
import numpy as np
from jax import lax
from jax.experimental import pallas as pl
from jax.experimental.pallas import tpu as pltpu

EPS = 1e-6
POOL_WINDOWS = (2, 4, 8, 16)
POOL_GROUP = 128
POOL_WIDTH = len(POOL_WINDOWS) * POOL_GROUP
POOL_HALO = 16
RET_HEADS = 6
RET_QK_DIM = 128
RET_V_DIM = 256
RET_CHUNK = 128
SGU_CHUNK = 128
SGU_GROUPS = 8
SGU_GROUP_DIM = 128
SGU_WIDTH = SGU_GROUPS * SGU_GROUP_DIM
DIFF_HEADS = 4
DIFF_HEAD_DIM = 128
DIFF_V_DIM = 2 * DIFF_HEAD_DIM
N_BUCKETS = 32
MAX_DISTANCE = 128
CONV_WIDTH = 3
NEG_INF = -1e30
LOG2E = 1.4426950408889634

V7X_VMEM_BYTES = 64 * 1024 * 1024
VMEM_LIMIT = V7X_VMEM_BYTES - 8 * 1024 * 1024
SUBLANES = 8

F32 = jnp.float32
BF16 = jnp.bfloat16


def _params(*semantics):
    return pltpu.CompilerParams(dimension_semantics=semantics, vmem_limit_bytes=VMEM_LIMIT)


def _rms(x, g):
    return x * lax.rsqrt(jnp.mean(x * x, axis=-1, keepdims=True) + EPS) * g


def _layer_norm(x, g):
    mu = jnp.mean(x, axis=-1, keepdims=True)
    xc = x - mu
    var = jnp.mean(xc * xc, axis=-1, keepdims=True)
    return xc * lax.rsqrt(var + EPS) * g


def _silu(x):
    return x * (1.0 / (1.0 + jnp.exp(-x)))


def _gelu_tanh(x):
    k = -2.0 * math.sqrt(2.0 / math.pi) * LOG2E
    return x / (1.0 + jnp.exp2(x * (k + (k * 0.044715) * (x * x))))


def _weight_rounding_specs(stacks, n_steps, step_of):
    in_specs, out_specs, out_shapes = [], [], []
    for w, layer in stacks:
        _, k, d = w.shape
        rows = k // n_steps
        assert rows * n_steps == k and rows % SUBLANES == 0
        in_specs.append(pl.BlockSpec((None, rows, d), lambda *g, layer=layer: (layer, step_of(*g), 0)))
        out_specs.append(pl.BlockSpec((rows, d), lambda *g: (step_of(*g), 0)))
        out_shapes.append(jax.ShapeDtypeStruct((k, d), BF16))
    return in_specs, out_specs, out_shapes


def _round_weight_slabs(w_refs, w_bf_refs):
    for w_ref, w_bf_ref in zip(w_refs, w_bf_refs):
        w_bf_ref[...] = w_ref[...].astype(BF16)


def _rmsnorm_kernel(x_ref, g_ref, o_ref):
    o_ref[...] = _rms(x_ref[...], g_ref[...]).astype(o_ref.dtype)


def rmsnorm(x, g, *, tm=512, out_dtype=BF16):
    m, d = x.shape
    return pl.pallas_call(
        _rmsnorm_kernel,
        grid=(m // tm,),
        in_specs=[pl.BlockSpec((tm, d), lambda i: (i, 0)),
                  pl.BlockSpec((1, d), lambda i: (0, 0))],
        out_specs=pl.BlockSpec((tm, d), lambda i: (i, 0)),
        out_shape=jax.ShapeDtypeStruct((m, d), out_dtype),
        compiler_params=_params("parallel"),
        name="rmsnorm",
    )(x, g.reshape(1, d))


def _matmul_kernel(x_ref, w_ref, o_ref, w_bf):
    @pl.when(pl.program_id(1) == 0)
    def _():
        w_bf[...] = w_ref[...].astype(BF16)

    o_ref[...] = jnp.dot(x_ref[...], w_bf[...], preferred_element_type=F32).astype(o_ref.dtype)


def matmul(x, w_stack, layer, *, tm=2048, tn=1024, out_dtype=BF16):
    m, k = x.shape
    n = w_stack.shape[2]
    return pl.pallas_call(
        _matmul_kernel,
        grid=(n // tn, m // tm),
        in_specs=[pl.BlockSpec((tm, k), lambda j, i: (i, 0)),
                  pl.BlockSpec((None, k, tn), lambda j, i: (layer, 0, j))],
        out_specs=pl.BlockSpec((tm, tn), lambda j, i: (i, j)),
        out_shape=jax.ShapeDtypeStruct((m, n), out_dtype),
        scratch_shapes=[pltpu.VMEM((k, tn), BF16)],
        compiler_params=_params("parallel", "arbitrary"),
        name="in_proj",
    )(x, w_stack)


PROJ_W_CHUNKS = 4


def _proj_res_norm_kernel(*refs, n_pieces, emit_h):
    x_refs = refs[:n_pieces]
    w_hbm, h_ref, g_ref = refs[n_pieces:n_pieces + 3]
    out_refs = refs[n_pieces + 3:-2]
    w_vmem, sems = refs[-2:]
    ck = w_vmem.shape[0] // PROJ_W_CHUNKS
    if n_pieces > 1:
        x = jnp.concatenate([r[...] for r in x_refs], axis=-1)
    else:
        x = x_refs[0][...]

    def chunk_copy(c):
        rows = pl.ds(c * ck, ck)
        return pltpu.make_async_copy(w_hbm.at[rows, :], w_vmem.at[rows, :], sems.at[c])

    def finish(acc):
        h_new = h_ref[...] + acc
        if emit_h:
            out_refs[0][...] = h_new
        out_refs[-1][...] = _rms(h_new, g_ref[...]).astype(out_refs[-1].dtype)

    @pl.when(pl.program_id(0) == 0)
    def _():
        for c in range(PROJ_W_CHUNKS):
            chunk_copy(c).start()
        acc = None
        for c in range(PROJ_W_CHUNKS):
            chunk_copy(c).wait()
            part = jnp.dot(x[:, c * ck:(c + 1) * ck], w_vmem[c * ck:(c + 1) * ck, :], preferred_element_type=F32)
            acc = part if acc is None else acc + part
        finish(acc)

    @pl.when(pl.program_id(0) > 0)
    def _():
        finish(jnp.dot(x, w_vmem[...], preferred_element_type=F32))


def proj_res_norm(pieces, w, h, g, *, tm, norm_dtype=BF16, emit_h=True):
    m, d = h.shape
    k = w.shape[0]
    assert sum(p.shape[1] for p in pieces) == k and k % (PROJ_W_CHUNKS * 128) == 0
    in_specs = [pl.BlockSpec((tm, p.shape[1]), lambda i: (i, 0)) for p in pieces]
    in_specs += [pl.BlockSpec(memory_space=pl.ANY),
                 pl.BlockSpec((tm, d), lambda i: (i, 0)),
                 pl.BlockSpec((1, d), lambda i: (0, 0))]
    row_spec = pl.BlockSpec((tm, d), lambda i: (i, 0))
    norm_shape = jax.ShapeDtypeStruct((m, d), norm_dtype)
    if emit_h:
        out_specs, out_shape = [row_spec, row_spec], [jax.ShapeDtypeStruct((m, d), F32), norm_shape]
    else:
        out_specs, out_shape = row_spec, norm_shape
    return pl.pallas_call(
        functools.partial(_proj_res_norm_kernel, n_pieces=len(pieces), emit_h=emit_h),
        grid=(m // tm,),
        in_specs=in_specs,
        out_specs=out_specs,
        out_shape=out_shape,
        scratch_shapes=[pltpu.VMEM((k, d), w.dtype), pltpu.SemaphoreType.DMA((PROJ_W_CHUNKS,))],
        compiler_params=_params("arbitrary"),
        name="proj_res_norm",
    )(*pieces, w, h, g.reshape(1, d))


def _shift_rows(a, prev, shift):
    rolled = pltpu.roll(a, shift, 0)
    row = lax.broadcasted_iota(jnp.int32, (SUBLANES, a.shape[1]), 0)
    head = rolled[:SUBLANES]
    for r in range(shift):
        head = jnp.where(row == r, prev[SUBLANES - shift + r:SUBLANES - shift + r + 1], head)
    return jnp.concatenate([head, rolled[SUBLANES:]], axis=0)


def _up_conv_gate_kernel(x_ref, wg_ref, wv_ref, cwg_ref, cwv_ref, cbg_ref, cbv_ref, o_ref,
                         wg_bf, wv_bf, carry, *, tiles_per_seq):
    i = pl.program_id(1)

    @pl.when(i == 0)
    def _():
        wg_bf[...] = wg_ref[...].astype(BF16)
        wv_bf[...] = wv_ref[...].astype(BF16)

    @pl.when(i % tiles_per_seq == 0)
    def _():
        carry[...] = jnp.zeros_like(carry)

    x = x_ref[...]
    tm = x_ref.shape[0]

    a_g = jnp.dot(x, wg_bf[...], preferred_element_type=F32)
    prev_g = carry[0]
    carry[0] = a_g[tm - SUBLANES:]
    cwg = cwg_ref[...]
    sg = _silu(a_g * cwg[2:3] + _shift_rows(a_g, prev_g, 1) * cwg[1:2] + _shift_rows(a_g, prev_g, 2) * cwg[0:1]
               + cbg_ref[...])
    cwv = cwv_ref[...]
    g2, g1, g0, gb = sg * cwv[2:3], sg * cwv[1:2], sg * cwv[0:1], sg * cbv_ref[...]

    a_v = jnp.dot(x, wv_bf[...], preferred_element_type=F32)
    prev_v = carry[1]
    carry[1] = a_v[tm - SUBLANES:]
    o_ref[...] = (a_v * g2 + _shift_rows(a_v, prev_v, 1) * g1 + _shift_rows(a_v, prev_v, 2) * g0
                  + gb).astype(o_ref.dtype)


def up_conv_gate(x, w_up_stack, layer, conv_w, conv_b, *, seq, tm=1024, tn=512):
    m, k = x.shape
    d_ff = w_up_stack.shape[2] // 2
    nj = d_ff // tn
    conv_b = conv_b.reshape(1, 2 * d_ff)
    return pl.pallas_call(
        functools.partial(_up_conv_gate_kernel, tiles_per_seq=seq // tm),
        grid=(nj, m // tm),
        in_specs=[pl.BlockSpec((tm, k), lambda j, i: (i, 0)),
                  pl.BlockSpec((None, k, tn), lambda j, i: (layer, 0, j)),
                  pl.BlockSpec((None, k, tn), lambda j, i: (layer, 0, nj + j)),
                  pl.BlockSpec((CONV_WIDTH, tn), lambda j, i: (0, j)),
                  pl.BlockSpec((CONV_WIDTH, tn), lambda j, i: (0, nj + j)),
                  pl.BlockSpec((1, tn), lambda j, i: (0, j)),
                  pl.BlockSpec((1, tn), lambda j, i: (0, nj + j))],
        out_specs=pl.BlockSpec((tm, tn), lambda j, i: (i, j)),
        out_shape=jax.ShapeDtypeStruct((m, d_ff), BF16),
        scratch_shapes=[pltpu.VMEM((k, tn), BF16), pltpu.VMEM((k, tn), BF16),
                        pltpu.VMEM((2, SUBLANES, tn), F32)],
        compiler_params=_params("parallel", "arbitrary"),
        name="up_conv_gate",
    )(x, w_up_stack, w_up_stack, conv_w, conv_w, conv_b, conv_b)


def _pool_retention_kernel(cd_ref, z_ref, cos_ref, sin_ref, intra_ref, qdec_ref, kdec_ref, poolw_ref,
                           pscale_ref, gn_ref, *rest, tile, n_weights):
    w_refs, (a_ref, b_ref), w_bf_refs = rest[:n_weights], rest[n_weights:n_weights + 2], rest[n_weights + 2:-2]
    u_carry, state = rest[-2:]
    _round_weight_slabs(w_refs, w_bf_refs)
    t = pl.program_id(1)

    @pl.when(t == 0)
    def _():
        u_carry[...] = jnp.zeros_like(u_carry)
        state[...] = jnp.zeros_like(state)

    u = z_ref[:, :POOL_WIDTH].astype(F32)
    ext = jnp.concatenate([u_carry[...], u], axis=0)
    u_carry[...] = u[tile - POOL_HALO:]
    pos = (t * tile + 1 + lax.broadcasted_iota(jnp.int32, (tile, POOL_GROUP), 0)).astype(F32)
    for gi, w in enumerate(POOL_WINDOWS):
        cols = slice(gi * POOL_GROUP, (gi + 1) * POOL_GROUP)
        s = ext[:, cols]
        span = 1
        while span < w:
            s = s + pltpu.roll(s, span, 0)
            span *= 2
        pooled = s[POOL_HALO:] / jnp.minimum(pos, float(w))
        y = (pooled - u[:, cols]).astype(BF16)
        a = jnp.dot(y, poolw_ref[gi], preferred_element_type=F32)
        a_ref[:, cols] = (a * pscale_ref[:, cols]).astype(a_ref.dtype)

    q0 = POOL_WIDTH
    k0 = q0 + RET_HEADS * RET_QK_DIM
    v0 = k0 + RET_HEADS * RET_QK_DIM
    g0 = v0 + RET_HEADS * RET_V_DIM
    half = RET_QK_DIM // 2

    def chunk_body(c, carry):
        r0 = pl.multiple_of(c * RET_CHUNK, RET_CHUNK)
        rows = pl.ds(r0, RET_CHUNK)
        cos = cos_ref[rows, :]
        sin = sin_ref[rows, :]
        for h in range(RET_HEADS):
            q = z_ref[rows, q0 + h * RET_QK_DIM:q0 + (h + 1) * RET_QK_DIM].astype(F32)
            k = z_ref[rows, k0 + h * RET_QK_DIM:k0 + (h + 1) * RET_QK_DIM].astype(F32)
            v = z_ref[rows, v0 + h * RET_V_DIM:v0 + (h + 1) * RET_V_DIM]
            gate = z_ref[rows, g0 + h * RET_V_DIM:g0 + (h + 1) * RET_V_DIM].astype(F32)
            qr = q * cos + pltpu.roll(q, half, 1) * sin
            kr = (k * cos + pltpu.roll(k, half, 1) * sin) * (RET_QK_DIM ** -0.5)
            scores = lax.dot_general(qr.astype(BF16), kr.astype(BF16), (((1,), (1,)), ((), ())),
                                     preferred_element_type=F32) * intra_ref[h]
            inner = jnp.dot(scores.astype(BF16), v, preferred_element_type=F32)
            st = state[h]
            cross = jnp.dot((qr * qdec_ref[h]).astype(BF16), st.astype(BF16), preferred_element_type=F32)
            kv = lax.dot_general((kr * kdec_ref[h]).astype(BF16), v, (((0,), (0,)), ((), ())),
                                 preferred_element_type=F32)
            state[h] = st * cd_ref[h] + kv
            vcols = slice(h * RET_V_DIM, (h + 1) * RET_V_DIM)
            r = _layer_norm(inner + cross, gn_ref[:, vcols])
            b_ref[rows, vcols] = (r * _silu(gate)).astype(b_ref.dtype)
        return carry

    lax.fori_loop(0, tile // RET_CHUNK, chunk_body, 0)


def _retention_tables(seq):
    f32 = np.float32
    half = RET_QK_DIM // 2
    inv = (1.0 / (10000.0 ** (np.arange(half, dtype=f32) / f32(half)))).astype(f32)
    ang = np.arange(seq, dtype=f32)[:, None] * inv[None, :]
    cos, sin = np.cos(ang), np.sin(ang)
    cos_full = np.concatenate([cos, cos], axis=-1)
    sin_signed = np.concatenate([-sin, sin], axis=-1)
    c = RET_CHUNK
    log_g = np.log(1.0 - 2.0 ** (-5.0 - np.arange(RET_HEADS, dtype=f32))).astype(f32)
    idx = np.arange(c, dtype=f32)
    diff = idx[:, None] - idx[None, :]
    intra = np.where(diff >= 0, np.exp(log_g[:, None, None] * np.maximum(diff, 0.0)), 0.0).astype(f32)
    q_dec = np.exp(log_g[:, None] * (idx[None, :] + 1.0)).astype(f32)
    k_dec = np.exp(log_g[:, None] * (c - 1.0 - idx[None, :])).astype(f32)
    chunk_dec = np.exp(log_g * c).astype(f32)
    q_dec = np.ascontiguousarray(np.broadcast_to(q_dec[:, :, None], (RET_HEADS, c, RET_QK_DIM)))
    k_dec = np.ascontiguousarray(np.broadcast_to(k_dec[:, :, None], (RET_HEADS, c, RET_QK_DIM)))
    return cos_full, sin_signed, intra, q_dec, k_dec, chunk_dec


def pool_retention(z, pool_w, pool_scale, ret_gn_g, round_weights, *, batch, seq, tile=512):
    m, width = z.shape
    nt = seq // tile
    w_in_specs, w_out_specs, w_out_shapes = _weight_rounding_specs(round_weights, batch * nt, lambda b, t: b * nt + t)
    cos, sin, intra, q_dec, k_dec, chunk_dec = _retention_tables(seq)
    vw = RET_HEADS * RET_V_DIM
    const3 = lambda b, t: (0, 0, 0)
    return pl.pallas_call(
        functools.partial(_pool_retention_kernel, tile=tile, n_weights=len(round_weights)),
        grid=(batch, nt),
        in_specs=[pl.BlockSpec(memory_space=pltpu.SMEM),
                  pl.BlockSpec((tile, width), lambda b, t: (b * nt + t, 0)),
                  pl.BlockSpec((tile, RET_QK_DIM), lambda b, t: (t, 0)),
                  pl.BlockSpec((tile, RET_QK_DIM), lambda b, t: (t, 0)),
                  pl.BlockSpec(intra.shape, const3),
                  pl.BlockSpec(q_dec.shape, const3),
                  pl.BlockSpec(k_dec.shape, const3),
                  pl.BlockSpec(pool_w.shape, const3),
                  pl.BlockSpec((1, POOL_WIDTH), lambda b, t: (0, 0)),
                  pl.BlockSpec((1, vw), lambda b, t: (0, 0))] + w_in_specs,
        out_specs=[pl.BlockSpec((tile, POOL_WIDTH), lambda b, t: (b * nt + t, 0)),
                   pl.BlockSpec((tile, vw), lambda b, t: (b * nt + t, 0))] + w_out_specs,
        out_shape=[jax.ShapeDtypeStruct((m, POOL_WIDTH), BF16), jax.ShapeDtypeStruct((m, vw), BF16)] + w_out_shapes,
        scratch_shapes=[pltpu.VMEM((POOL_HALO, POOL_WIDTH), F32),
                        pltpu.VMEM((RET_HEADS, RET_QK_DIM, RET_V_DIM), F32)],
        compiler_params=_params("parallel", "arbitrary"),
        name="pool_retention",
    )(chunk_dec, z, cos, sin, intra, q_dec, k_dec, pool_w.astype(BF16),
      pool_scale.reshape(1, POOL_WIDTH), ret_gn_g.reshape(1, vw), *[w for w, _ in round_weights])


def _sgu_kernel(z_ref, lng_ref, ws_ref, bs_ref, o_ref, *, tile):
    zc = _gelu_tanh(z_ref[...].astype(F32))
    zu = zc[:, :SGU_WIDTH]
    v = _layer_norm(zc[:, SGU_WIDTH:], lng_ref[...]).astype(BF16)
    row = lax.broadcasted_iota(jnp.int32, (SGU_CHUNK, SGU_CHUNK), 0)
    col = lax.broadcasted_iota(jnp.int32, (SGU_CHUNK, SGU_CHUNK), 1)
    for g in range(SGU_GROUPS):
        wm = jnp.where(row >= col, ws_ref[g], 0.0).astype(BF16)
        bias = bs_ref[g]
        cols = slice(g * SGU_GROUP_DIM, (g + 1) * SGU_GROUP_DIM)
        for c in range(tile // SGU_CHUNK):
            rows = slice(c * SGU_CHUNK, (c + 1) * SGU_CHUNK)
            sv = jnp.dot(wm, v[rows, cols], preferred_element_type=F32) + bias
            o_ref[rows, cols] = (zu[rows, cols] * sv).astype(o_ref.dtype)


def spatial_gating(z, ln_g, w_s, b_s, *, tile=512):
    m = z.shape[0]
    return pl.pallas_call(
        functools.partial(_sgu_kernel, tile=tile),
        grid=(m // tile,),
        in_specs=[pl.BlockSpec((tile, 2 * SGU_WIDTH), lambda i: (i, 0)),
                  pl.BlockSpec((1, SGU_WIDTH), lambda i: (0, 0)),
                  pl.BlockSpec(w_s.shape, lambda i: (0, 0, 0)),
                  pl.BlockSpec((SGU_GROUPS, SGU_CHUNK, 1), lambda i: (0, 0, 0))],
        out_specs=pl.BlockSpec((tile, SGU_WIDTH), lambda i: (i, 0)),
        out_shape=jax.ShapeDtypeStruct((m, SGU_WIDTH), BF16),
        compiler_params=_params("parallel"),
        name="spatial_gating",
    )(z, ln_g.reshape(1, SGU_WIDTH), w_s, b_s.reshape(SGU_GROUPS, SGU_CHUNK, 1))


def _t5_bucket(rel):
    n = np.maximum(rel, 0)
    max_exact = N_BUCKETS // 2
    large = max_exact + (np.log(np.maximum(n, 1).astype(np.float32) / np.float32(max_exact))
                         / np.float32(math.log(MAX_DISTANCE / max_exact))
                         * np.float32(N_BUCKETS - max_exact)).astype(np.int32)
    large = np.minimum(large, N_BUCKETS - 1)
    return np.where(n < max_exact, n, large).astype(np.int32)


def _bias_tiles_kernel(relb_ref, bucket_ref, o_ref):
    h = pl.program_id(0)
    bucket = bucket_ref[...]
    acc = jnp.zeros(bucket.shape, F32)
    for b in range(N_BUCKETS):
        acc = jnp.where(bucket == b, relb_ref[b, h], acc)
    far_bias = relb_ref[N_BUCKETS - 1, h]
    o_ref[0] = (acc - far_bias) * LOG2E


def rel_bias_tiles(rel_bias, *, tile):
    qk = np.arange(tile)[:, None] - np.arange(tile)[None, :]
    bucket = np.stack([_t5_bucket(qk), _t5_bucket(tile + qk)])
    return pl.pallas_call(
        _bias_tiles_kernel,
        grid=(DIFF_HEADS,),
        in_specs=[pl.BlockSpec(memory_space=pltpu.SMEM),
                  pl.BlockSpec((2, tile, tile), lambda h: (0, 0, 0))],
        out_specs=pl.BlockSpec((1, 2, tile, tile), lambda h: (h, 0, 0, 0)),
        out_shape=jax.ShapeDtypeStruct((DIFF_HEADS, 2, tile, tile), F32),
        compiler_params=_params("parallel"),
        name="rel_bias_tiles",
    )(rel_bias, bucket)


def _diff_attn_kernel(lq1_ref, lk1_ref, lq2_ref, lk2_ref, q_ref, k_ref, v_ref, bias_ref, subg_ref,
                      *rest, tile, seq, lam_init, n_weights):
    w_refs, o_ref, w_bf_refs = rest[:n_weights], rest[n_weights], rest[n_weights + 1:]
    _round_weight_slabs(w_refs, w_bf_refs)
    dh = DIFF_HEAD_DIM
    lam = (jnp.exp(jnp.sum(lq1_ref[...] * lk1_ref[...], axis=-1, keepdims=True))
           - jnp.exp(jnp.sum(lq2_ref[...] * lk2_ref[...], axis=-1, keepdims=True)) + lam_init)
    row = lax.broadcasted_iota(jnp.int32, (tile, tile), 0)
    col = lax.broadcasted_iota(jnp.int32, (tile, tile), 1)
    causal = row >= col

    def query_tile(qi):
        kv_len = (qi + 1) * tile
        rows = slice(qi * tile, kv_len)
        q = (q_ref[rows, :].astype(F32) * (dh ** -0.5 * LOG2E)).astype(BF16)
        streams = []
        for s in range(2):
            sc = lax.dot_general(q[:, s * dh:(s + 1) * dh], k_ref[:kv_len, s * dh:(s + 1) * dh],
                                 (((1,), (1,)), ((), ())), preferred_element_type=F32)
            parts = []
            if qi >= 2:
                parts.append(sc[:, :kv_len - 2 * tile])
            if qi >= 1:
                parts.append(sc[:, kv_len - 2 * tile:kv_len - tile] + bias_ref[0, 1])
            parts.append(jnp.where(causal, sc[:, kv_len - tile:] + bias_ref[0, 0], NEG_INF))
            sc = jnp.concatenate(parts, axis=1) if len(parts) > 1 else parts[0]
            p = jnp.exp2(sc - jnp.max(sc, axis=-1, keepdims=True))
            l = jnp.sum(p, axis=-1, keepdims=True)
            streams.append(jnp.dot(p.astype(BF16), v_ref[:kv_len, :], preferred_element_type=F32) / l)
        d = streams[0] - lam * streams[1]
        o_ref[rows, :] = (_rms(d, subg_ref[...]) * (1.0 - lam_init)).astype(o_ref.dtype)

    n_tiles = seq // tile
    always = pl.program_id(0) >= 0
    for first in range(n_tiles // 2):
        @pl.when(always)
        def _(first=first):
            query_tile(first)
            query_tile(n_tiles - 1 - first)


def diff_attention(z, bias_tiles, lq1, lk1, lq2, lk2, subln_g, round_weights, *, batch, seq, layer_idx,
                   tile):
    m = z.shape[0]
    assert tile >= MAX_DISTANCE and seq % (2 * tile) == 0
    w_in_specs, w_out_specs, w_out_shapes = _weight_rounding_specs(
        round_weights, batch * DIFF_HEADS, lambda b, h: b * DIFF_HEADS + h)
    w = 2 * DIFF_HEAD_DIM
    q_blk = 2 * SGU_WIDTH // w
    k_blk = q_blk + DIFF_HEADS
    v_blk = k_blk + DIFF_HEADS
    lam_init = 0.8 - 0.6 * math.exp(-0.3 * layer_idx)
    vec = lambda a: a.reshape(1, DIFF_HEAD_DIM)
    vec_spec = pl.BlockSpec((1, DIFF_HEAD_DIM), lambda b, h: (0, 0))
    return pl.pallas_call(
        functools.partial(_diff_attn_kernel, tile=tile, seq=seq, lam_init=lam_init, n_weights=len(round_weights)),
        grid=(batch, DIFF_HEADS),
        in_specs=[vec_spec, vec_spec, vec_spec, vec_spec,
                  pl.BlockSpec((seq, w), lambda b, h: (b, q_blk + h)),
                  pl.BlockSpec((seq, w), lambda b, h: (b, k_blk + h)),
                  pl.BlockSpec((seq, w), lambda b, h: (b, v_blk + h)),
                  pl.BlockSpec((1, 2, tile, tile), lambda b, h: (h, 0, 0, 0)),
                  pl.BlockSpec((1, DIFF_V_DIM), lambda b, h: (0, 0))] + w_in_specs,
        out_specs=[pl.BlockSpec((seq, DIFF_V_DIM), lambda b, h: (b, h))] + w_out_specs,
        out_shape=[jax.ShapeDtypeStruct((m, DIFF_HEADS * DIFF_V_DIM), BF16)] + w_out_shapes,
        compiler_params=_params("parallel", "parallel"),
        name="diff_attention",
    )(vec(lq1), vec(lk1), vec(lq2), vec(lk2), z, z, z, bias_tiles, subln_g.reshape(1, DIFF_V_DIM),
      *[w for w, _ in round_weights])


ATTN_TILE = 256
ROW_TILE_OUT = 512
ROW_TILE_DOWN = 256


def kernel(x, w_in_even, w_out_even, pool_w, pool_scale, ret_gn_g, w_in_odd, w_out_odd, sgu_ln_g, sgu_w, sgu_b,
           lam_q1, lam_k1, lam_q2, lam_k2, diff_subln_g, rel_bias, mix_norm_g, ffn_norm_g, w_up, conv_w, conv_b,
           w_down, final_norm_g):
    batch, seq, d = x.shape
    depth = mix_norm_g.shape[0]
    h = x.reshape(batch * seq, d)
    hn = rmsnorm(h, mix_norm_g[0])
    bias_tiles = rel_bias_tiles(rel_bias, tile=ATTN_TILE)
    out = None
    for i in range(depth):
        if i % 2 == 0:
            e = i // 2
            z = matmul(hn, w_in_even, e)
            a_out, b_out, w_out, w_dn = pool_retention(z, pool_w[e], pool_scale[e], ret_gn_g[e],
                                                        [(w_out_even, e), (w_down, i)], batch=batch, seq=seq)
            pieces = [a_out, b_out]
        else:
            o = i // 2
            z = matmul(hn, w_in_odd, o)
            c_out = spatial_gating(z, sgu_ln_g[o], sgu_w[o], sgu_b[o])
            d_out, w_out, w_dn = diff_attention(z, bias_tiles, lam_q1[o], lam_k1[o], lam_q2[o], lam_k2[o],
                                                diff_subln_g[o], [(w_out_odd, o), (w_down, i)],
                                                batch=batch, seq=seq, layer_idx=i, tile=ATTN_TILE)
            pieces = [c_out, d_out]
        h, hn = proj_res_norm(pieces, w_out, h, ffn_norm_g[i], tm=ROW_TILE_OUT)
        act = up_conv_gate(hn, w_up, i, conv_w[i], conv_b[i], seq=seq)
        if i + 1 < depth:
            h, hn = proj_res_norm([act], w_dn, h, mix_norm_g[i + 1], tm=ROW_TILE_DOWN)
        else:
            out = proj_res_norm([act], w_dn, h, final_norm_g, tm=ROW_TILE_DOWN,
                                norm_dtype=x.dtype, emit_h=False)
    return out.reshape(batch, seq, d)
```

```python
import functools
import math

import jax
import jax.numpy as jnp
import numpy as np
from jax import lax
from jax.experimental import pallas as pl
from jax.experimental.pallas import tpu as pltpu

EPS = 1e-6
POOL_WINDOWS = (2, 4, 8, 16)
POOL_GROUP = 128
POOL_WIDTH = len(POOL_WINDOWS) * POOL_GROUP
POOL_HALO = 16
RET_HEADS = 6
RET_QK_DIM = 128
RET_V_DIM = 256
RET_CHUNK = 128
SGU_CHUNK = 128
SGU_GROUPS = 8
SGU_GROUP_DIM = 128
SGU_WIDTH = SGU_GROUPS * SGU_GROUP_DIM
DIFF_HEADS = 4
DIFF_HEAD_DIM = 128
DIFF_V_DIM = 2 * DIFF_HEAD_DIM
N_BUCKETS = 32
MAX_DISTANCE = 128
CONV_WIDTH = 3
NEG_INF = -1e30
LOG2E = 1.4426950408889634

V7X_VMEM_BYTES = 64 * 1024 * 1024
VMEM_LIMIT = V7X_VMEM_BYTES - 8 * 1024 * 1024
SUBLANES = 8

F32 = jnp.float32
BF16 = jnp.bfloat16


def _params(*semantics):
    return pltpu.CompilerParams(dimension_semantics=semantics, vmem_limit_bytes=VMEM_LIMIT)


def _rms(x, g):
    return x * lax.rsqrt(jnp.mean(x * x, axis=-1, keepdims=True) + EPS) * g


def _layer_norm(x, g):
    mu = jnp.mean(x, axis=-1, keepdims=True)
    xc = x - mu
    var = jnp.mean(xc * xc, axis=-1, keepdims=True)
    return xc * lax.rsqrt(var + EPS) * g


def _silu(x):
    return x * (1.0 / (1.0 + jnp.exp(-x)))


def _gelu_tanh(x):
    k = -2.0 * math.sqrt(2.0 / math.pi) * LOG2E
    return x / (1.0 + jnp.exp2(x * (k + (k * 0.044715) * (x * x))))


def _weight_rounding_specs(stacks, n_steps, step_of):
    in_specs, out_specs, out_shapes = [], [], []
    for w, layer in stacks:
        _, k, d = w.shape
        rows = k // n_steps
        assert rows * n_steps == k and rows % SUBLANES == 0
        in_specs.append(pl.BlockSpec((None, rows, d), lambda *g, layer=layer: (layer, step_of(*g), 0)))
        out_specs.append(pl.BlockSpec((rows, d), lambda *g: (step_of(*g), 0)))
        out_shapes.append(jax.ShapeDtypeStruct((k, d), BF16))
    return in_specs, out_specs, out_shapes


def _round_weight_slabs(w_refs, w_bf_refs):
    for w_ref, w_bf_ref in zip(w_refs, w_bf_refs):
        w_bf_ref[...] = w_ref[...].astype(BF16)


def _rmsnorm_kernel(x_ref, g_ref, o_ref):
    o_ref[...] = _rms(x_ref[...], g_ref[...]).astype(o_ref.dtype)


def rmsnorm(x, g, *, tm=1024, out_dtype=BF16):
    m, d = x.shape
    return pl.pallas_call(
        _rmsnorm_kernel,
        grid=(m // tm,),
        in_specs=[pl.BlockSpec((tm, d), lambda i: (i, 0)),
                  pl.BlockSpec((1, d), lambda i: (0, 0))],
        out_specs=pl.BlockSpec((tm, d), lambda i: (i, 0)),
        out_shape=jax.ShapeDtypeStruct((m, d), out_dtype),
        compiler_params=_params("parallel"),
        name="rmsnorm",
    )(x, g.reshape(1, d))


def _matmul_kernel(x_ref, w_ref, o_ref, w_bf):
    @pl.when(pl.program_id(1) == 0)
    def _():
        w_bf[...] = w_ref[...].astype(BF16)

    o_ref[...] = jnp.dot(x_ref[...], w_bf[...], preferred_element_type=F32).astype(o_ref.dtype)


def matmul(x, w_stack, layer, *, tm=2048, tn=1024, out_dtype=BF16):
    m, k = x.shape
    n = w_stack.shape[2]
    return pl.pallas_call(
        _matmul_kernel,
        grid=(n // tn, m // tm),
        in_specs=[pl.BlockSpec((tm, k), lambda j, i: (i, 0)),
                  pl.BlockSpec((None, k, tn), lambda j, i: (layer, 0, j))],
        out_specs=pl.BlockSpec((tm, tn), lambda j, i: (i, j)),
        out_shape=jax.ShapeDtypeStruct((m, n), out_dtype),
        scratch_shapes=[pltpu.VMEM((k, tn), BF16)],
        compiler_params=_params("parallel", "arbitrary"),
        name="in_proj",
    )(x, w_stack)


def _proj_res_norm_kernel(*refs, n_pieces, emit_h):
    x_refs = refs[:n_pieces]
    w_ref, h_ref, g_ref = refs[n_pieces:n_pieces + 3]
    out_refs = refs[n_pieces + 3:]
    if n_pieces > 1:
        x = jnp.concatenate([r[...] for r in x_refs], axis=-1)
    else:
        x = x_refs[0][...]
    h_new = h_ref[...] + jnp.dot(x, w_ref[...], preferred_element_type=F32)
    if emit_h:
        out_refs[0][...] = h_new
    out_refs[-1][...] = _rms(h_new, g_ref[...]).astype(out_refs[-1].dtype)


def proj_res_norm(pieces, w, h, g, *, tm, norm_dtype=BF16, emit_h=True):
    m, d = h.shape
    k = w.shape[0]
    assert sum(p.shape[1] for p in pieces) == k
    in_specs = [pl.BlockSpec((tm, p.shape[1]), lambda i: (i, 0)) for p in pieces]
    in_specs += [pl.BlockSpec((k, d), lambda i: (0, 0), pipeline_mode=pl.Buffered(1)),
                 pl.BlockSpec((tm, d), lambda i: (i, 0)),
                 pl.BlockSpec((1, d), lambda i: (0, 0))]
    row_spec = pl.BlockSpec((tm, d), lambda i: (i, 0))
    norm_shape = jax.ShapeDtypeStruct((m, d), norm_dtype)
    if emit_h:
        out_specs, out_shape = [row_spec, row_spec], [jax.ShapeDtypeStruct((m, d), F32), norm_shape]
    else:
        out_specs, out_shape = row_spec, norm_shape
    return pl.pallas_call(
        functools.partial(_proj_res_norm_kernel, n_pieces=len(pieces), emit_h=emit_h),
        grid=(m // tm,),
        in_specs=in_specs,
        out_specs=out_specs,
        out_shape=out_shape,
        compiler_params=_params("parallel"),
        name="proj_res_norm",
    )(*pieces, w, h, g.reshape(1, d))


def _shift_rows(a, prev, shift):
    rolled = pltpu.roll(a, shift, 0)
    row = lax.broadcasted_iota(jnp.int32, (SUBLANES, a.shape[1]), 0)
    head = rolled[:SUBLANES]
    for r in range(shift):
        head = jnp.where(row == r, prev[SUBLANES - shift + r:SUBLANES - shift + r + 1], head)
    return jnp.concatenate([head, rolled[SUBLANES:]], axis=0)


def _up_conv_gate_kernel(x_ref, wg_ref, wv_ref, cwg_ref, cwv_ref, cbg_ref, cbv_ref, o_ref,
                         wg_bf, wv_bf, carry, *, tiles_per_seq):
    i = pl.program_id(1)

    @pl.when(i == 0)
    def _():
        wg_bf[...] = wg_ref[...].astype(BF16)
        wv_bf[...] = wv_ref[...].astype(BF16)

    @pl.when(i % tiles_per_seq == 0)
    def _():
        carry[...] = jnp.zeros_like(carry)

    x = x_ref[...]
    tm = x_ref.shape[0]

    a_g = jnp.dot(x, wg_bf[...], preferred_element_type=F32)
    prev_g = carry[0]
    carry[0] = a_g[tm - SUBLANES:]
    cwg = cwg_ref[...]
    sg = _silu(a_g * cwg[2:3] + _shift_rows(a_g, prev_g, 1) * cwg[1:2] + _shift_rows(a_g, prev_g, 2) * cwg[0:1]
               + cbg_ref[...])
    cwv = cwv_ref[...]
    g2, g1, g0, gb = sg * cwv[2:3], sg * cwv[1:2], sg * cwv[0:1], sg * cbv_ref[...]

    a_v = jnp.dot(x, wv_bf[...], preferred_element_type=F32)
    prev_v = carry[1]
    carry[1] = a_v[tm - SUBLANES:]
    o_ref[...] = (a_v * g2 + _shift_rows(a_v, prev_v, 1) * g1 + _shift_rows(a_v, prev_v, 2) * g0
                  + gb).astype(o_ref.dtype)


def up_conv_gate(x, w_up_stack, layer, conv_w, conv_b, *, seq, tm=1024, tn=512):
    m, k = x.shape
    d_ff = w_up_stack.shape[2] // 2
    nj = d_ff // tn
    conv_b = conv_b.reshape(1, 2 * d_ff)
    return pl.pallas_call(
        functools.partial(_up_conv_gate_kernel, tiles_per_seq=seq // tm),
        grid=(nj, m // tm),
        in_specs=[pl.BlockSpec((tm, k), lambda j, i: (i, 0)),
                  pl.BlockSpec((None, k, tn), lambda j, i: (layer, 0, j)),
                  pl.BlockSpec((None, k, tn), lambda j, i: (layer, 0, nj + j)),
                  pl.BlockSpec((CONV_WIDTH, tn), lambda j, i: (0, j)),
                  pl.BlockSpec((CONV_WIDTH, tn), lambda j, i: (0, nj + j)),
                  pl.BlockSpec((1, tn), lambda j, i: (0, j)),
                  pl.BlockSpec((1, tn), lambda j, i: (0, nj + j))],
        out_specs=pl.BlockSpec((tm, tn), lambda j, i: (i, j)),
        out_shape=jax.ShapeDtypeStruct((m, d_ff), BF16),
        scratch_shapes=[pltpu.VMEM((k, tn), BF16), pltpu.VMEM((k, tn), BF16),
                        pltpu.VMEM((2, SUBLANES, tn), F32)],
        compiler_params=_params("parallel", "arbitrary"),
        name="up_conv_gate",
    )(x, w_up_stack, w_up_stack, conv_w, conv_w, conv_b, conv_b)


def _pool_retention_kernel(cd_ref, z_ref, cos_ref, sin_ref, intra_ref, qdec_ref, kdec_ref, poolw_ref,
                           pscale_ref, gn_ref, *rest, tile, n_weights):
    w_refs, (a_ref, b_ref), w_bf_refs = rest[:n_weights], rest[n_weights:n_weights + 2], rest[n_weights + 2:-2]
    u_carry, state = rest[-2:]
    _round_weight_slabs(w_refs, w_bf_refs)
    t = pl.program_id(1)

    @pl.when(t == 0)
    def _():
        u_carry[...] = jnp.zeros_like(u_carry)
        state[...] = jnp.zeros_like(state)

    u = z_ref[:, :POOL_WIDTH].astype(F32)
    ext = jnp.concatenate([u_carry[...], u], axis=0)
    u_carry[...] = u[tile - POOL_HALO:]
    pos = (t * tile + 1 + lax.broadcasted_iota(jnp.int32, (tile, POOL_GROUP), 0)).astype(F32)
    for gi, w in enumerate(POOL_WINDOWS):
        cols = slice(gi * POOL_GROUP, (gi + 1) * POOL_GROUP)
        s = ext[:, cols]
        span = 1
        while span < w:
            s = s + pltpu.roll(s, span, 0)
            span *= 2
        pooled = s[POOL_HALO:] / jnp.minimum(pos, float(w))
        y = (pooled - u[:, cols]).astype(BF16)
        a = jnp.dot(y, poolw_ref[gi], preferred_element_type=F32)
        a_ref[:, cols] = (a * pscale_ref[:, cols]).astype(a_ref.dtype)

    q0 = POOL_WIDTH
    k0 = q0 + RET_HEADS * RET_QK_DIM
    v0 = k0 + RET_HEADS * RET_QK_DIM
    g0 = v0 + RET_HEADS * RET_V_DIM
    half = RET_QK_DIM // 2

    def chunk_body(c, carry):
        r0 = pl.multiple_of(c * RET_CHUNK, RET_CHUNK)
        rows = pl.ds(r0, RET_CHUNK)
        cos = cos_ref[rows, :]
        sin = sin_ref[rows, :]
        for h in range(RET_HEADS):
            q = z_ref[rows, q0 + h * RET_QK_DIM:q0 + (h + 1) * RET_QK_DIM].astype(F32)
            k = z_ref[rows, k0 + h * RET_QK_DIM:k0 + (h + 1) * RET_QK_DIM].astype(F32)
            v = z_ref[rows, v0 + h * RET_V_DIM:v0 + (h + 1) * RET_V_DIM]
            gate = z_ref[rows, g0 + h * RET_V_DIM:g0 + (h + 1) * RET_V_DIM].astype(F32)
            qr = q * cos + pltpu.roll(q, half, 1) * sin
            kr = (k * cos + pltpu.roll(k, half, 1) * sin) * (RET_QK_DIM ** -0.5)
            scores = lax.dot_general(qr.astype(BF16), kr.astype(BF16), (((1,), (1,)), ((), ())),
                                     preferred_element_type=F32) * intra_ref[h]
            inner = jnp.dot(scores.astype(BF16), v, preferred_element_type=F32)
            st = state[h]
            cross = jnp.dot((qr * qdec_ref[h]).astype(BF16), st.astype(BF16), preferred_element_type=F32)
            kv = lax.dot_general((kr * kdec_ref[h]).astype(BF16), v, (((0,), (0,)), ((), ())),
                                 preferred_element_type=F32)
            state[h] = st * cd_ref[h] + kv
            vcols = slice(h * RET_V_DIM, (h + 1) * RET_V_DIM)
            r = _layer_norm(inner + cross, gn_ref[:, vcols])
            b_ref[rows, vcols] = (r * _silu(gate)).astype(b_ref.dtype)
        return carry

    lax.fori_loop(0, tile // RET_CHUNK, chunk_body, 0)


def _retention_tables(seq):
    f32 = np.float32
    half = RET_QK_DIM // 2
    inv = (1.0 / (10000.0 ** (np.arange(half, dtype=f32) / f32(half)))).astype(f32)
    ang = np.arange(seq, dtype=f32)[:, None] * inv[None, :]
    cos, sin = np.cos(ang), np.sin(ang)
    cos_full = np.concatenate([cos, cos], axis=-1)
    sin_signed = np.concatenate([-sin, sin], axis=-1)
    c = RET_CHUNK
    log_g = np.log(1.0 - 2.0 ** (-5.0 - np.arange(RET_HEADS, dtype=f32))).astype(f32)
    idx = np.arange(c, dtype=f32)
    diff = idx[:, None] - idx[None, :]
    intra = np.where(diff >= 0, np.exp(log_g[:, None, None] * np.maximum(diff, 0.0)), 0.0).astype(f32)
    q_dec = np.exp(log_g[:, None] * (idx[None, :] + 1.0)).astype(f32)
    k_dec = np.exp(log_g[:, None] * (c - 1.0 - idx[None, :])).astype(f32)
    chunk_dec = np.exp(log_g * c).astype(f32)
    q_dec = np.ascontiguousarray(np.broadcast_to(q_dec[:, :, None], (RET_HEADS, c, RET_QK_DIM)))
    k_dec = np.ascontiguousarray(np.broadcast_to(k_dec[:, :, None], (RET_HEADS, c, RET_QK_DIM)))
    return cos_full, sin_signed, intra, q_dec, k_dec, chunk_dec


def pool_retention(z, pool_w, pool_scale, ret_gn_g, round_weights, *, batch, seq, tile=512):
    m, width = z.shape
    nt = seq // tile
    w_in_specs, w_out_specs, w_out_shapes = _weight_rounding_specs(round_weights, batch * nt, lambda b, t: b * nt + t)
    cos, sin, intra, q_dec, k_dec, chunk_dec = _retention_tables(seq)
    vw = RET_HEADS * RET_V_DIM
    const3 = lambda b, t: (0, 0, 0)
    return pl.pallas_call(
        functools.partial(_pool_retention_kernel, tile=tile, n_weights=len(round_weights)),
        grid=(batch, nt),
        in_specs=[pl.BlockSpec(memory_space=pltpu.SMEM),
                  pl.BlockSpec((tile, width), lambda b, t: (b * nt + t, 0)),
                  pl.BlockSpec((tile, RET_QK_DIM), lambda b, t: (t, 0)),
                  pl.BlockSpec((tile, RET_QK_DIM), lambda b, t: (t, 0)),
                  pl.BlockSpec(intra.shape, const3),
                  pl.BlockSpec(q_dec.shape, const3),
                  pl.BlockSpec(k_dec.shape, const3),
                  pl.BlockSpec(pool_w.shape, const3),
                  pl.BlockSpec((1, POOL_WIDTH), lambda b, t: (0, 0)),
                  pl.BlockSpec((1, vw), lambda b, t: (0, 0))] + w_in_specs,
        out_specs=[pl.BlockSpec((tile, POOL_WIDTH), lambda b, t: (b * nt + t, 0)),
                   pl.BlockSpec((tile, vw), lambda b, t: (b * nt + t, 0))] + w_out_specs,
        out_shape=[jax.ShapeDtypeStruct((m, POOL_WIDTH), BF16), jax.ShapeDtypeStruct((m, vw), BF16)] + w_out_shapes,
        scratch_shapes=[pltpu.VMEM((POOL_HALO, POOL_WIDTH), F32),
                        pltpu.VMEM((RET_HEADS, RET_QK_DIM, RET_V_DIM), F32)],
        compiler_params=_params("parallel", "arbitrary"),
        name="pool_retention",
    )(chunk_dec, z, cos, sin, intra, q_dec, k_dec, pool_w.astype(BF16),
      pool_scale.reshape(1, POOL_WIDTH), ret_gn_g.reshape(1, vw), *[w for w, _ in round_weights])


def _sgu_kernel(z_ref, lng_ref, ws_ref, bs_ref, o_ref, *, tile):
    zc = _gelu_tanh(z_ref[...].astype(F32))
    zu = zc[:, :SGU_WIDTH]
    v = _layer_norm(zc[:, SGU_WIDTH:], lng_ref[...]).astype(BF16)
    row = lax.broadcasted_iota(jnp.int32, (SGU_CHUNK, SGU_CHUNK), 0)
    col = lax.broadcasted_iota(jnp.int32, (SGU_CHUNK, SGU_CHUNK), 1)
    for g in range(SGU_GROUPS):
        wm = jnp.where(row >= col, ws_ref[g], 0.0).astype(BF16)
        bias = bs_ref[g]
        cols = slice(g * SGU_GROUP_DIM, (g + 1) * SGU_GROUP_DIM)
        for c in range(tile // SGU_CHUNK):
            rows = slice(c * SGU_CHUNK, (c + 1) * SGU_CHUNK)
            sv = jnp.dot(wm, v[rows, cols], preferred_element_type=F32) + bias
            o_ref[rows, cols] = (zu[rows, cols] * sv).astype(o_ref.dtype)


def spatial_gating(z, ln_g, w_s, b_s, *, tile=1024):
    m = z.shape[0]
    return pl.pallas_call(
        functools.partial(_sgu_kernel, tile=tile),
        grid=(m // tile,),
        in_specs=[pl.BlockSpec((tile, 2 * SGU_WIDTH), lambda i: (i, 0)),
                  pl.BlockSpec((1, SGU_WIDTH), lambda i: (0, 0)),
                  pl.BlockSpec(w_s.shape, lambda i: (0, 0, 0)),
                  pl.BlockSpec((SGU_GROUPS, SGU_CHUNK, 1), lambda i: (0, 0, 0))],
        out_specs=pl.BlockSpec((tile, SGU_WIDTH), lambda i: (i, 0)),
        out_shape=jax.ShapeDtypeStruct((m, SGU_WIDTH), BF16),
        compiler_params=_params("parallel"),
        name="spatial_gating",
    )(z, ln_g.reshape(1, SGU_WIDTH), w_s, b_s.reshape(SGU_GROUPS, SGU_CHUNK, 1))


def _t5_bucket(rel):
    n = np.maximum(rel, 0)
    max_exact = N_BUCKETS // 2
    large = max_exact + (np.log(np.maximum(n, 1).astype(np.float32) / np.float32(max_exact))
                         / np.float32(math.log(MAX_DISTANCE / max_exact))
                         * np.float32(N_BUCKETS - max_exact)).astype(np.int32)
    large = np.minimum(large, N_BUCKETS - 1)
    return np.where(n < max_exact, n, large).astype(np.int32)


def _bias_tiles_kernel(relb_ref, bucket_ref, o_ref):
    h = pl.program_id(0)
    bucket = bucket_ref[...]
    acc = jnp.zeros(bucket.shape, F32)
    for b in range(N_BUCKETS):
        acc = jnp.where(bucket == b, relb_ref[b, h], acc)
    far_bias = relb_ref[N_BUCKETS - 1, h]
    o_ref[0] = (acc - far_bias) * LOG2E


def rel_bias_tiles(rel_bias, *, tile):
    qk = np.arange(tile)[:, None] - np.arange(tile)[None, :]
    bucket = np.stack([_t5_bucket(qk), _t5_bucket(tile + qk)])
    return pl.pallas_call(
        _bias_tiles_kernel,
        grid=(DIFF_HEADS,),
        in_specs=[pl.BlockSpec(memory_space=pltpu.SMEM),
                  pl.BlockSpec((2, tile, tile), lambda h: (0, 0, 0))],
        out_specs=pl.BlockSpec((1, 2, tile, tile), lambda h: (h, 0, 0, 0)),
        out_shape=jax.ShapeDtypeStruct((DIFF_HEADS, 2, tile, tile), F32),
        compiler_params=_params("parallel"),
        name="rel_bias_tiles",
    )(rel_bias, bucket)


def _diff_attn_kernel(lq1_ref, lk1_ref, lq2_ref, lk2_ref, q_ref, k_ref, v_ref, bias_ref, subg_ref,
                      *rest, tile, seq, lam_init, n_weights):
    w_refs, o_ref, w_bf_refs = rest[:n_weights], rest[n_weights], rest[n_weights + 1:]
    _round_weight_slabs(w_refs, w_bf_refs)
    dh = DIFF_HEAD_DIM
    lam = (jnp.exp(jnp.sum(lq1_ref[...] * lk1_ref[...], axis=-1, keepdims=True))
           - jnp.exp(jnp.sum(lq2_ref[...] * lk2_ref[...], axis=-1, keepdims=True)) + lam_init)
    row = lax.broadcasted_iota(jnp.int32, (tile, tile), 0)
    col = lax.broadcasted_iota(jnp.int32, (tile, tile), 1)
    causal = row >= col

    def query_tile(qi):
        kv_len = (qi + 1) * tile
        rows = slice(qi * tile, kv_len)
        q = (q_ref[rows, :].astype(F32) * (dh ** -0.5 * LOG2E)).astype(BF16)
        streams = []
        for s in range(2):
            sc = lax.dot_general(q[:, s * dh:(s + 1) * dh], k_ref[:kv_len, s * dh:(s + 1) * dh],
                                 (((1,), (1,)), ((), ())), preferred_element_type=F32)
            parts = []
            if qi >= 2:
                parts.append(sc[:, :kv_len - 2 * tile])
            if qi >= 1:
                parts.append(sc[:, kv_len - 2 * tile:kv_len - tile] + bias_ref[0, 1])
            parts.append(jnp.where(causal, sc[:, kv_len - tile:] + bias_ref[0, 0], NEG_INF))
            sc = jnp.concatenate(parts, axis=1) if len(parts) > 1 else parts[0]
            p = jnp.exp2(sc - jnp.max(sc, axis=-1, keepdims=True))
            l = jnp.sum(p, axis=-1, keepdims=True)
            streams.append(jnp.dot(p.astype(BF16), v_ref[:kv_len, :], preferred_element_type=F32) / l)
        d = streams[0] - lam * streams[1]
        o_ref[rows, :] = (_rms(d, subg_ref[...]) * (1.0 - lam_init)).astype(o_ref.dtype)

    n_tiles = seq // tile
    always = pl.program_id(0) >= 0
    for first in range(n_tiles // 2):
        @pl.when(always)
        def _(first=first):
            query_tile(first)
            query_tile(n_tiles - 1 - first)


def diff_attention(z, bias_tiles, lq1, lk1, lq2, lk2, subln_g, round_weights, *, batch, seq, layer_idx,
                   tile):
    m = z.shape[0]
    assert tile >= MAX_DISTANCE and seq % (2 * tile) == 0
    w_in_specs, w_out_specs, w_out_shapes = _weight_rounding_specs(
        round_weights, batch * DIFF_HEADS, lambda b, h: b * DIFF_HEADS + h)
    w = 2 * DIFF_HEAD_DIM
    q_blk = 2 * SGU_WIDTH // w
    k_blk = q_blk + DIFF_HEADS
    v_blk = k_blk + DIFF_HEADS
    lam_init = 0.8 - 0.6 * math.exp(-0.3 * layer_idx)
    vec = lambda a: a.reshape(1, DIFF_HEAD_DIM)
    vec_spec = pl.BlockSpec((1, DIFF_HEAD_DIM), lambda b, h: (0, 0))
    return pl.pallas_call(
        functools.partial(_diff_attn_kernel, tile=tile, seq=seq, lam_init=lam_init, n_weights=len(round_weights)),
        grid=(batch, DIFF_HEADS),
        in_specs=[vec_spec, vec_spec, vec_spec, vec_spec,
                  pl.BlockSpec((seq, w), lambda b, h: (b, q_blk + h)),
                  pl.BlockSpec((seq, w), lambda b, h: (b, k_blk + h)),
                  pl.BlockSpec((seq, w), lambda b, h: (b, v_blk + h)),
                  pl.BlockSpec((1, 2, tile, tile), lambda b, h: (h, 0, 0, 0)),
                  pl.BlockSpec((1, DIFF_V_DIM), lambda b, h: (0, 0))] + w_in_specs,
        out_specs=[pl.BlockSpec((seq, DIFF_V_DIM), lambda b, h: (b, h))] + w_out_specs,
        out_shape=[jax.ShapeDtypeStruct((m, DIFF_HEADS * DIFF_V_DIM), BF16)] + w_out_shapes,
        compiler_params=_params("parallel", "parallel"),
        name="diff_attention",
    )(vec(lq1), vec(lk1), vec(lq2), vec(lk2), z, z, z, bias_tiles, subln_g.reshape(1, DIFF_V_DIM),
      *[w for w, _ in round_weights])


ATTN_TILE = 256
ROW_TILE_OUT = 512
ROW_TILE_DOWN = 256


def kernel(x, w_in_even, w_out_even, pool_w, pool_scale, ret_gn_g, w_in_odd, w_out_odd, sgu_ln_g, sgu_w, sgu_b,
           lam_q1, lam_k1, lam_q2, lam_k2, diff_subln_g, rel_bias, mix_norm_g, ffn_norm_g, w_up, conv_w, conv_b,
           w_down, final_norm_g):
    batch, seq, d = x.shape
    depth = mix_norm_g.shape[0]
    h = x.reshape(batch * seq, d)
    hn = rmsnorm(h, mix_norm_g[0])
    bias_tiles = rel_bias_tiles(rel_bias, tile=ATTN_TILE)
    out = None
    for i in range(depth):
        if i % 2 == 0:
            e = i // 2
            z = matmul(hn, w_in_even, e)
            a_out, b_out, w_out, w_dn = pool_retention(z, pool_w[e], pool_scale[e], ret_gn_g[e],
                                                        [(w_out_even, e), (w_down, i)], batch=batch, seq=seq)
            pieces = [a_out, b_out]
        else:
            o = i // 2
            z = matmul(hn, w_in_odd, o)
            c_out = spatial_gating(z, sgu_ln_g[o], sgu_w[o], sgu_b[o])
            d_out, w_out, w_dn = diff_attention(z, bias_tiles, lam_q1[o], lam_k1[o], lam_q2[o], lam_k2[o],
                                                diff_subln_g[o], [(w_out_odd, o), (w_down, i)],
                                                batch=batch, seq=seq, layer_idx=i, tile=ATTN_TILE)
            pieces = [c_out, d_out]
        h, hn = proj_res_norm(pieces, w_out, h, ffn_norm_g[i], tm=ROW_TILE_OUT)
        act = up_conv_gate(hn, w_up, i, conv_w[i], conv_b[i], seq=seq)
        if i + 1 < depth:
            h, hn = proj_res_norm([act], w_dn, h, mix_norm_g[i + 1], tm=ROW_TILE_DOWN)
        else:
            out = proj_res_norm([act], w_dn, h, final_norm_g, tm=ROW_TILE_DOWN,
                                norm_dtype=x.dtype, emit_h=False)
    return out.reshape(batch, seq, d)
```

```python
import functools
import math

import jax
import jax.numpy as jnp
import numpy as np
from jax import lax
from jax.experimental import pallas as pl
from jax.experimental.pallas import tpu as pltpu

EPS = 1e-6
POOL_WINDOWS = (2, 4, 8, 16)
POOL_GROUP = 128
POOL_WIDTH = len(POOL_WINDOWS) * POOL_GROUP
POOL_HALO = 16
RET_HEADS = 6
RET_QK_DIM = 128
RET_V_DIM = 256
RET_CHUNK = 128
SGU_CHUNK = 128
SGU_GROUPS = 8
SGU_GROUP_DIM = 128
SGU_WIDTH = SGU_GROUPS * SGU_GROUP_DIM
DIFF_HEADS = 4
DIFF_HEAD_DIM = 128
DIFF_V_DIM = 2 * DIFF_HEAD_DIM
N_BUCKETS = 32
MAX_DISTANCE = 128
CONV_WIDTH = 3
NEG_INF = -1e30
LOG2E = 1.4426950408889634

V7X_VMEM_BYTES = 64 * 1024 * 1024
VMEM_LIMIT = V7X_VMEM_BYTES - 8 * 1024 * 1024
SUBLANES = 8

F32 = jnp.float32
BF16 = jnp.bfloat16


def _params(*semantics):
    return pltpu.CompilerParams(dimension_semantics=semantics, vmem_limit_bytes=VMEM_LIMIT)


def _rms(x, g):
    return x * lax.rsqrt(jnp.mean(x * x, axis=-1, keepdims=True) + EPS) * g


def _layer_norm(x, g):
    mu = jnp.mean(x, axis=-1, keepdims=True)
    xc = x - mu
    var = jnp.mean(xc * xc, axis=-1, keepdims=True)
    return xc * lax.rsqrt(var + EPS) * g


def _silu(x):
    return x * (1.0 / (1.0 + jnp.exp(-x)))


def _gelu_tanh(x):
    k = -2.0 * math.sqrt(2.0 / math.pi) * LOG2E
    return x / (1.0 + jnp.exp2(x * (k + (k * 0.044715) * (x * x))))


def _weight_rounding_specs(stacks, n_steps, step_of):
    in_specs, out_specs, out_shapes = [], [], []
    for w, layer in stacks:
        _, k, d = w.shape
        rows = k // n_steps
        assert rows * n_steps == k and rows % SUBLANES == 0
        in_specs.append(pl.BlockSpec((None, rows, d), lambda *g, layer=layer: (layer, step_of(*g), 0)))
        out_specs.append(pl.BlockSpec((rows, d), lambda *g: (step_of(*g), 0)))
        out_shapes.append(jax.ShapeDtypeStruct((k, d), BF16))
    return in_specs, out_specs, out_shapes


def _round_weight_slabs(w_refs, w_bf_refs):
    for w_ref, w_bf_ref in zip(w_refs, w_bf_refs):
        w_bf_ref[...] = w_ref[...].astype(BF16)


def _rmsnorm_kernel(x_ref, g_ref, o_ref):
    o_ref[...] = _rms(x_ref[...], g_ref[...]).astype(o_ref.dtype)


def rmsnorm(x, g, *, tm=1024, out_dtype=BF16):
    m, d = x.shape
    return pl.pallas_call(
        _rmsnorm_kernel,
        grid=(m // tm,),
        in_specs=[pl.BlockSpec((tm, d), lambda i: (i, 0)),
                  pl.BlockSpec((1, d), lambda i: (0, 0))],
        out_specs=pl.BlockSpec((tm, d), lambda i: (i, 0)),
        out_shape=jax.ShapeDtypeStruct((m, d), out_dtype),
        compiler_params=_params("parallel"),
        name="rmsnorm",
    )(x, g.reshape(1, d))


def _matmul_kernel(x_ref, w_ref, o_ref, w_bf):
    @pl.when(pl.program_id(1) == 0)
    def _():
        w_bf[...] = w_ref[...].astype(BF16)

    o_ref[...] = jnp.dot(x_ref[...], w_bf[...], preferred_element_type=F32).astype(o_ref.dtype)


def matmul(x, w_stack, layer, *, tm=2048, tn=1024, out_dtype=BF16):
    m, k = x.shape
    n = w_stack.shape[2]
    return pl.pallas_call(
        _matmul_kernel,
        grid=(n // tn, m // tm),
        in_specs=[pl.BlockSpec((tm, k), lambda j, i: (i, 0)),
                  pl.BlockSpec((None, k, tn), lambda j, i: (layer, 0, j))],
        out_specs=pl.BlockSpec((tm, tn), lambda j, i: (i, j)),
        out_shape=jax.ShapeDtypeStruct((m, n), out_dtype),
        scratch_shapes=[pltpu.VMEM((k, tn), BF16)],
        compiler_params=_params("parallel", "arbitrary"),
        name="in_proj",
    )(x, w_stack)


def _proj_res_norm_kernel(*refs, n_pieces, emit_h):
    x_refs = refs[:n_pieces]
    w_ref, h_ref, g_ref = refs[n_pieces:n_pieces + 3]
    out_refs = refs[n_pieces + 3:]
    if n_pieces > 1:
        x = jnp.concatenate([r[...] for r in x_refs], axis=-1)
    else:
        x = x_refs[0][...]
    h_new = h_ref[...] + jnp.dot(x, w_ref[...], preferred_element_type=F32)
    if emit_h:
        out_refs[0][...] = h_new
    out_refs[-1][...] = _rms(h_new, g_ref[...]).astype(out_refs[-1].dtype)


def proj_res_norm(pieces, w, h, g, *, tm, norm_dtype=BF16, emit_h=True):
    m, d = h.shape
    k = w.shape[0]
    assert sum(p.shape[1] for p in pieces) == k
    in_specs = [pl.BlockSpec((tm, p.shape[1]), lambda i: (i, 0)) for p in pieces]
    in_specs += [pl.BlockSpec((k, d), lambda i: (0, 0), pipeline_mode=pl.Buffered(1)),
                 pl.BlockSpec((tm, d), lambda i: (i, 0)),
                 pl.BlockSpec((1, d), lambda i: (0, 0))]
    row_spec = pl.BlockSpec((tm, d), lambda i: (i, 0))
    norm_shape = jax.ShapeDtypeStruct((m, d), norm_dtype)
    if emit_h:
        out_specs, out_shape = [row_spec, row_spec], [jax.ShapeDtypeStruct((m, d), F32), norm_shape]
    else:
        out_specs, out_shape = row_spec, norm_shape
    return pl.pallas_call(
        functools.partial(_proj_res_norm_kernel, n_pieces=len(pieces), emit_h=emit_h),
        grid=(m // tm,),
        in_specs=in_specs,
        out_specs=out_specs,
        out_shape=out_shape,
        compiler_params=_params("parallel"),
        name="proj_res_norm",
    )(*pieces, w, h, g.reshape(1, d))


def _shift_rows(a, prev, shift):
    rolled = pltpu.roll(a, shift, 0)
    row = lax.broadcasted_iota(jnp.int32, (SUBLANES, a.shape[1]), 0)
    head = rolled[:SUBLANES]
    for r in range(shift):
        head = jnp.where(row == r, prev[SUBLANES - shift + r:SUBLANES - shift + r + 1], head)
    return jnp.concatenate([head, rolled[SUBLANES:]], axis=0)


def _up_conv_gate_kernel(x_ref, wg_ref, wv_ref, cwg_ref, cwv_ref, cbg_ref, cbv_ref, o_ref,
                         wg_bf, wv_bf, carry, *, tiles_per_seq):
    i = pl.program_id(1)

    @pl.when(i == 0)
    def _():
        wg_bf[...] = wg_ref[...].astype(BF16)
        wv_bf[...] = wv_ref[...].astype(BF16)

    @pl.when(i % tiles_per_seq == 0)
    def _():
        carry[...] = jnp.zeros_like(carry)

    x = x_ref[...]
    tm = x_ref.shape[0]

    a_g = jnp.dot(x, wg_bf[...], preferred_element_type=F32)
    prev_g = carry[0]
    carry[0] = a_g[tm - SUBLANES:]
    cwg = cwg_ref[...]
    sg = _silu(a_g * cwg[2:3] + _shift_rows(a_g, prev_g, 1) * cwg[1:2] + _shift_rows(a_g, prev_g, 2) * cwg[0:1]
               + cbg_ref[...])
    cwv = cwv_ref[...]
    g2, g1, g0, gb = sg * cwv[2:3], sg * cwv[1:2], sg * cwv[0:1], sg * cbv_ref[...]

    half = tm // 2
    wv = wv_bf[...]
    prev_v = carry[1]
    for r in range(2):
        rows = slice(r * half, (r + 1) * half)
        a_v = jnp.dot(x[rows], wv, preferred_element_type=F32)
        o_ref[rows, :] = (a_v * g2[rows] + _shift_rows(a_v, prev_v, 1) * g1[rows]
                          + _shift_rows(a_v, prev_v, 2) * g0[rows] + gb[rows]).astype(o_ref.dtype)
        prev_v = a_v[half - SUBLANES:]
        if r == 0:
            bits = pltpu.bitcast(a_v[:2 * SUBLANES, :128], jnp.uint32)
            zero = pltpu.bitcast((bits >> 16) >> 16, F32).astype(BF16)
            top = jnp.concatenate([wv[:2 * SUBLANES, :128] + zero, wv[:2 * SUBLANES, 128:]], axis=1)
            wv = jnp.concatenate([top, wv[2 * SUBLANES:]], axis=0)
    carry[1] = prev_v


def up_conv_gate(x, w_up_stack, layer, conv_w, conv_b, *, seq, tm=1024, tn=512):
    m, k = x.shape
    d_ff = w_up_stack.shape[2] // 2
    nj = d_ff // tn
    conv_b = conv_b.reshape(1, 2 * d_ff)
    return pl.pallas_call(
        functools.partial(_up_conv_gate_kernel, tiles_per_seq=seq // tm),
        grid=(nj, m // tm),
        in_specs=[pl.BlockSpec((tm, k), lambda j, i: (i, 0)),
                  pl.BlockSpec((None, k, tn), lambda j, i: (layer, 0, j)),
                  pl.BlockSpec((None, k, tn), lambda j, i: (layer, 0, nj + j)),
                  pl.BlockSpec((CONV_WIDTH, tn), lambda j, i: (0, j)),
                  pl.BlockSpec((CONV_WIDTH, tn), lambda j, i: (0, nj + j)),
                  pl.BlockSpec((1, tn), lambda j, i: (0, j)),
                  pl.BlockSpec((1, tn), lambda j, i: (0, nj + j))],
        out_specs=pl.BlockSpec((tm, tn), lambda j, i: (i, j)),
        out_shape=jax.ShapeDtypeStruct((m, d_ff), BF16),
        scratch_shapes=[pltpu.VMEM((k, tn), BF16), pltpu.VMEM((k, tn), BF16),
                        pltpu.VMEM((2, SUBLANES, tn), F32)],
        compiler_params=_params("parallel", "arbitrary"),
        name="up_conv_gate",
    )(x, w_up_stack, w_up_stack, conv_w, conv_w, conv_b, conv_b)


def _pool_retention_kernel(cd_ref, z_ref, cos_ref, sin_ref, intra_ref, qdec_ref, kdec_ref, poolw_ref,
                           pscale_ref, gn_ref, *rest, tile, n_weights):
    w_refs, (a_ref, b_ref), w_bf_refs = rest[:n_weights], rest[n_weights:n_weights + 2], rest[n_weights + 2:-2]
    u_carry, state = rest[-2:]
    _round_weight_slabs(w_refs, w_bf_refs)
    t = pl.program_id(1)

    @pl.when(t == 0)
    def _():
        u_carry[...] = jnp.zeros_like(u_carry)
        state[...] = jnp.zeros_like(state)

    u = z_ref[:, :POOL_WIDTH].astype(F32)
    ext = jnp.concatenate([u_carry[...], u], axis=0)
    u_carry[...] = u[tile - POOL_HALO:]
    pos = (t * tile + 1 + lax.broadcasted_iota(jnp.int32, (tile, POOL_GROUP), 0)).astype(F32)
    for gi, w in enumerate(POOL_WINDOWS):
        cols = slice(gi * POOL_GROUP, (gi + 1) * POOL_GROUP)
        s = ext[:, cols]
        span = 1
        while span < w:
            s = s + pltpu.roll(s, span, 0)
            span *= 2
        pooled = s[POOL_HALO:] / jnp.minimum(pos, float(w))
        y = (pooled - u[:, cols]).astype(BF16)
        a = jnp.dot(y, poolw_ref[gi], preferred_element_type=F32)
        a_ref[:, cols] = (a * pscale_ref[:, cols]).astype(a_ref.dtype)

    q0 = POOL_WIDTH
    k0 = q0 + RET_HEADS * RET_QK_DIM
    v0 = k0 + RET_HEADS * RET_QK_DIM
    g0 = v0 + RET_HEADS * RET_V_DIM
    half = RET_QK_DIM // 2

    def chunk_body(c, carry):
        r0 = pl.multiple_of(c * RET_CHUNK, RET_CHUNK)
        rows = pl.ds(r0, RET_CHUNK)
        cos = cos_ref[rows, :]
        sin = sin_ref[rows, :]
        for h in range(RET_HEADS):
            q = z_ref[rows, q0 + h * RET_QK_DIM:q0 + (h + 1) * RET_QK_DIM].astype(F32)
            k = z_ref[rows, k0 + h * RET_QK_DIM:k0 + (h + 1) * RET_QK_DIM].astype(F32)
            v = z_ref[rows, v0 + h * RET_V_DIM:v0 + (h + 1) * RET_V_DIM]
            gate = z_ref[rows, g0 + h * RET_V_DIM:g0 + (h + 1) * RET_V_DIM].astype(F32)
            qr = q * cos + pltpu.roll(q, half, 1) * sin
            kr = (k * cos + pltpu.roll(k, half, 1) * sin) * (RET_QK_DIM ** -0.5)
            scores = lax.dot_general(qr.astype(BF16), kr.astype(BF16), (((1,), (1,)), ((), ())),
                                     preferred_element_type=F32) * intra_ref[h]
            inner = jnp.dot(scores.astype(BF16), v, preferred_element_type=F32)
            st = state[h]
            cross = jnp.dot((qr * qdec_ref[h]).astype(BF16), st.astype(BF16), preferred_element_type=F32)
            kv = lax.dot_general((kr * kdec_ref[h]).astype(BF16), v, (((0,), (0,)), ((), ())),
                                 preferred_element_type=F32)
            state[h] = st * cd_ref[h] + kv
            vcols = slice(h * RET_V_DIM, (h + 1) * RET_V_DIM)
            r = _layer_norm(inner + cross, gn_ref[:, vcols])
            b_ref[rows, vcols] = (r * _silu(gate)).astype(b_ref.dtype)
        return carry

    lax.fori_loop(0, tile // RET_CHUNK, chunk_body, 0, unroll=2)


def _retention_tables(seq):
    f32 = np.float32
    half = RET_QK_DIM // 2
    inv = (1.0 / (10000.0 ** (np.arange(half, dtype=f32) / f32(half)))).astype(f32)
    ang = np.arange(seq, dtype=f32)[:, None] * inv[None, :]
    cos, sin = np.cos(ang), np.sin(ang)
    cos_full = np.concatenate([cos, cos], axis=-1)
    sin_signed = np.concatenate([-sin, sin], axis=-1)
    c = RET_CHUNK
    log_g = np.log(1.0 - 2.0 ** (-5.0 - np.arange(RET_HEADS, dtype=f32))).astype(f32)
    idx = np.arange(c, dtype=f32)
    diff = idx[:, None] - idx[None, :]
    intra = np.where(diff >= 0, np.exp(log_g[:, None, None] * np.maximum(diff, 0.0)), 0.0).astype(f32)
    q_dec = np.exp(log_g[:, None] * (idx[None, :] + 1.0)).astype(f32)
    k_dec = np.exp(log_g[:, None] * (c - 1.0 - idx[None, :])).astype(f32)
    chunk_dec = np.exp(log_g * c).astype(f32)
    q_dec = np.ascontiguousarray(np.broadcast_to(q_dec[:, :, None], (RET_HEADS, c, RET_QK_DIM)))
    k_dec = np.ascontiguousarray(np.broadcast_to(k_dec[:, :, None], (RET_HEADS, c, RET_QK_DIM)))
    return cos_full, sin_signed, intra, q_dec, k_dec, chunk_dec


def pool_retention(z, pool_w, pool_scale, ret_gn_g, round_weights, *, batch, seq, tile=512):
    m, width = z.shape
    nt = seq // tile
    w_in_specs, w_out_specs, w_out_shapes = _weight_rounding_specs(round_weights, batch * nt, lambda b, t: b * nt + t)
    cos, sin, intra, q_dec, k_dec, chunk_dec = _retention_tables(seq)
    vw = RET_HEADS * RET_V_DIM
    const3 = lambda b, t: (0, 0, 0)
    return pl.pallas_call(
        functools.partial(_pool_retention_kernel, tile=tile, n_weights=len(round_weights)),
        grid=(batch, nt),
        in_specs=[pl.BlockSpec(memory_space=pltpu.SMEM),
                  pl.BlockSpec((tile, width), lambda b, t: (b * nt + t, 0)),
                  pl.BlockSpec((tile, RET_QK_DIM), lambda b, t: (t, 0)),
                  pl.BlockSpec((tile, RET_QK_DIM), lambda b, t: (t, 0)),
                  pl.BlockSpec(intra.shape, const3),
                  pl.BlockSpec(q_dec.shape, const3),
                  pl.BlockSpec(k_dec.shape, const3),
                  pl.BlockSpec(pool_w.shape, const3),
                  pl.BlockSpec((1, POOL_WIDTH), lambda b, t: (0, 0)),
                  pl.BlockSpec((1, vw), lambda b, t: (0, 0))] + w_in_specs,
        out_specs=[pl.BlockSpec((tile, POOL_WIDTH), lambda b, t: (b * nt + t, 0)),
                   pl.BlockSpec((tile, vw), lambda b, t: (b * nt + t, 0))] + w_out_specs,
        out_shape=[jax.ShapeDtypeStruct((m, POOL_WIDTH), BF16), jax.ShapeDtypeStruct((m, vw), BF16)] + w_out_shapes,
        scratch_shapes=[pltpu.VMEM((POOL_HALO, POOL_WIDTH), F32),
                        pltpu.VMEM((RET_HEADS, RET_QK_DIM, RET_V_DIM), F32)],
        compiler_params=_params("parallel", "arbitrary"),
        name="pool_retention",
    )(chunk_dec, z, cos, sin, intra, q_dec, k_dec, pool_w.astype(BF16),
      pool_scale.reshape(1, POOL_WIDTH), ret_gn_g.reshape(1, vw), *[w for w, _ in round_weights])


def _sgu_kernel(z_ref, lng_ref, ws_ref, bs_ref, o_ref, *, tile):
    zc = _gelu_tanh(z_ref[...].astype(F32))
    zu = zc[:, :SGU_WIDTH]
    v = _layer_norm(zc[:, SGU_WIDTH:], lng_ref[...]).astype(BF16)
    row = lax.broadcasted_iota(jnp.int32, (SGU_CHUNK, SGU_CHUNK), 0)
    col = lax.broadcasted_iota(jnp.int32, (SGU_CHUNK, SGU_CHUNK), 1)
    for g in range(SGU_GROUPS):
        wm = jnp.where(row >= col, ws_ref[g], 0.0).astype(BF16)
        bias = bs_ref[g]
        cols = slice(g * SGU_GROUP_DIM, (g + 1) * SGU_GROUP_DIM)
        for c in range(tile // SGU_CHUNK):
            rows = slice(c * SGU_CHUNK, (c + 1) * SGU_CHUNK)
            sv = jnp.dot(wm, v[rows, cols], preferred_element_type=F32) + bias
            o_ref[rows, cols] = (zu[rows, cols] * sv).astype(o_ref.dtype)


def spatial_gating(z, ln_g, w_s, b_s, *, tile=1024):
    m = z.shape[0]
    return pl.pallas_call(
        functools.partial(_sgu_kernel, tile=tile),
        grid=(m // tile,),
        in_specs=[pl.BlockSpec((tile, 2 * SGU_WIDTH), lambda i: (i, 0)),
                  pl.BlockSpec((1, SGU_WIDTH), lambda i: (0, 0)),
                  pl.BlockSpec(w_s.shape, lambda i: (0, 0, 0)),
                  pl.BlockSpec((SGU_GROUPS, SGU_CHUNK, 1), lambda i: (0, 0, 0))],
        out_specs=pl.BlockSpec((tile, SGU_WIDTH), lambda i: (i, 0)),
        out_shape=jax.ShapeDtypeStruct((m, SGU_WIDTH), BF16),
        compiler_params=_params("parallel"),
        name="spatial_gating",
    )(z, ln_g.reshape(1, SGU_WIDTH), w_s, b_s.reshape(SGU_GROUPS, SGU_CHUNK, 1))


def _t5_bucket(rel):
    n = np.maximum(rel, 0)
    max_exact = N_BUCKETS // 2
    large = max_exact + (np.log(np.maximum(n, 1).astype(np.float32) / np.float32(max_exact))
                         / np.float32(math.log(MAX_DISTANCE / max_exact))
                         * np.float32(N_BUCKETS - max_exact)).astype(np.int32)
    large = np.minimum(large, N_BUCKETS - 1)
    return np.where(n < max_exact, n, large).astype(np.int32)


def _bias_tiles_kernel(relb_ref, bucket_ref, o_ref):
    h = pl.program_id(0)
    bucket = bucket_ref[...]
    acc = jnp.zeros(bucket.shape, F32)
    for b in range(N_BUCKETS):
        acc = jnp.where(bucket == b, relb_ref[b, h], acc)
    far_bias = relb_ref[N_BUCKETS - 1, h]
    o_ref[0] = (acc - far_bias) * LOG2E


def rel_bias_tiles(rel_bias, *, tile):
    qk = np.arange(tile)[:, None] - np.arange(tile)[None, :]
    bucket = np.stack([_t5_bucket(qk), _t5_bucket(tile + qk)])
    return pl.pallas_call(
        _bias_tiles_kernel,
        grid=(DIFF_HEADS,),
        in_specs=[pl.BlockSpec(memory_space=pltpu.SMEM),
                  pl.BlockSpec((2, tile, tile), lambda h: (0, 0, 0))],
        out_specs=pl.BlockSpec((1, 2, tile, tile), lambda h: (h, 0, 0, 0)),
        out_shape=jax.ShapeDtypeStruct((DIFF_HEADS, 2, tile, tile), F32),
        compiler_params=_params("parallel"),
        name="rel_bias_tiles",
    )(rel_bias, bucket)


def _diff_attn_kernel(lq1_ref, lk1_ref, lq2_ref, lk2_ref, q_ref, k_ref, v_ref, bias_ref, subg_ref,
                      *rest, tile, seq, lam_init, n_weights):
    w_refs, o_ref, w_bf_refs = rest[:n_weights], rest[n_weights], rest[n_weights + 1:]
    _round_weight_slabs(w_refs, w_bf_refs)
    dh = DIFF_HEAD_DIM
    lam = (jnp.exp(jnp.sum(lq1_ref[...] * lk1_ref[...], axis=-1, keepdims=True))
           - jnp.exp(jnp.sum(lq2_ref[...] * lk2_ref[...], axis=-1, keepdims=True)) + lam_init)
    row = lax.broadcasted_iota(jnp.int32, (tile, tile), 0)
    col = lax.broadcasted_iota(jnp.int32, (tile, tile), 1)
    causal = row >= col

    def query_tile(qi):
        kv_len = (qi + 1) * tile
        rows = slice(qi * tile, kv_len)
        q = (q_ref[rows, :].astype(F32) * (dh ** -0.5 * LOG2E)).astype(BF16)
        streams = []
        for s in range(2):
            sc = lax.dot_general(q[:, s * dh:(s + 1) * dh], k_ref[:kv_len, s * dh:(s + 1) * dh],
                                 (((1,), (1,)), ((), ())), preferred_element_type=F32)
            parts = []
            if qi >= 2:
                parts.append(sc[:, :kv_len - 2 * tile])
            if qi >= 1:
                parts.append(sc[:, kv_len - 2 * tile:kv_len - tile] + bias_ref[0, 1])
            parts.append(jnp.where(causal, sc[:, kv_len - tile:] + bias_ref[0, 0], NEG_INF))
            sc = jnp.concatenate(parts, axis=1) if len(parts) > 1 else parts[0]
            p = jnp.exp2(sc - jnp.max(sc, axis=-1, keepdims=True))
            l = jnp.sum(p, axis=-1, keepdims=True)
            streams.append(jnp.dot(p.astype(BF16), v_ref[:kv_len, :], preferred_element_type=F32) / l)
        d = streams[0] - lam * streams[1]
        o_ref[rows, :] = (_rms(d, subg_ref[...]) * (1.0 - lam_init)).astype(o_ref.dtype)

    n_tiles = seq // tile
    always = pl.program_id(0) >= 0
    for first in range(n_tiles // 2):
        @pl.when(always)
        def _(first=first):
            query_tile(first)
            query_tile(n_tiles - 1 - first)


def diff_attention(z, bias_tiles, lq1, lk1, lq2, lk2, subln_g, round_weights, *, batch, seq, layer_idx,
                   tile):
    m = z.shape[0]
    assert tile >= MAX_DISTANCE and seq % (2 * tile) == 0
    w_in_specs, w_out_specs, w_out_shapes = _weight_rounding_specs(
        round_weights, batch * DIFF_HEADS, lambda b, h: b * DIFF_HEADS + h)
    w = 2 * DIFF_HEAD_DIM
    q_blk = 2 * SGU_WIDTH // w
    k_blk = q_blk + DIFF_HEADS
    v_blk = k_blk + DIFF_HEADS
    lam_init = 0.8 - 0.6 * math.exp(-0.3 * layer_idx)
    vec = lambda a: a.reshape(1, DIFF_HEAD_DIM)
    vec_spec = pl.BlockSpec((1, DIFF_HEAD_DIM), lambda b, h: (0, 0))
    return pl.pallas_call(
        functools.partial(_diff_attn_kernel, tile=tile, seq=seq, lam_init=lam_init, n_weights=len(round_weights)),
        grid=(batch, DIFF_HEADS),
        in_specs=[vec_spec, vec_spec, vec_spec, vec_spec,
                  pl.BlockSpec((seq, w), lambda b, h: (b, q_blk + h)),
                  pl.BlockSpec((seq, w), lambda b, h: (b, k_blk + h)),
                  pl.BlockSpec((seq, w), lambda b, h: (b, v_blk + h)),
                  pl.BlockSpec((1, 2, tile, tile), lambda b, h: (h, 0, 0, 0)),
                  pl.BlockSpec((1, DIFF_V_DIM), lambda b, h: (0, 0))] + w_in_specs,
        out_specs=[pl.BlockSpec((seq, DIFF_V_DIM), lambda b, h: (b, h))] + w_out_specs,
        out_shape=[jax.ShapeDtypeStruct((m, DIFF_HEADS * DIFF_V_DIM), BF16)] + w_out_shapes,
        compiler_params=_params("parallel", "parallel"),
        name="diff_attention",
    )(vec(lq1), vec(lk1), vec(lq2), vec(lk2), z, z, z, bias_tiles, subln_g.reshape(1, DIFF_V_DIM),
      *[w for w, _ in round_weights])


ATTN_TILE = 256
ROW_TILE_OUT = 512
ROW_TILE_DOWN = 256


def kernel(x, w_in_even, w_out_even, pool_w, pool_scale, ret_gn_g, w_in_odd, w_out_odd, sgu_ln_g, sgu_w, sgu_b,
           lam_q1, lam_k1, lam_q2, lam_k2, diff_subln_g, rel_bias, mix_norm_g, ffn_norm_g, w_up, conv_w, conv_b,
           w_down, final_norm_g):
    batch, seq, d = x.shape
    depth = mix_norm_g.shape[0]
    h = x.reshape(batch * seq, d)
    hn = rmsnorm(h, mix_norm_g[0])
    bias_tiles = rel_bias_tiles(rel_bias, tile=ATTN_TILE)
    out = None
    for i in range(depth):
        if i % 2 == 0:
            e = i // 2
            z = matmul(hn, w_in_even, e)
            a_out, b_out, w_out, w_dn = pool_retention(z, pool_w[e], pool_scale[e], ret_gn_g[e],
                                                        [(w_out_even, e), (w_down, i)], batch=batch, seq=seq)
            pieces = [a_out, b_out]
        else:
            o = i // 2
            z = matmul(hn, w_in_odd, o)
            c_out = spatial_gating(z, sgu_ln_g[o], sgu_w[o], sgu_b[o])
            d_out, w_out, w_dn = diff_attention(z, bias_tiles, lam_q1[o], lam_k1[o], lam_q2[o], lam_k2[o],
                                                diff_subln_g[o], [(w_out_odd, o), (w_down, i)],
                                                batch=batch, seq=seq, layer_idx=i, tile=ATTN_TILE)
            pieces = [c_out, d_out]
        h, hn = proj_res_norm(pieces, w_out, h, ffn_norm_g[i], tm=ROW_TILE_OUT)
        act = up_conv_gate(hn, w_up, i, conv_w[i], conv_b[i], seq=seq)
        if i + 1 < depth:
            h, hn = proj_res_norm([act], w_dn, h, mix_norm_g[i + 1], tm=ROW_TILE_DOWN)
        else:
            out = proj_res_norm([act], w_dn, h, final_norm_g, tm=ROW_TILE_DOWN,
                                norm_dtype=x.dtype, emit_h=False)
    return out.reshape(batch, seq, d)
```

```python
import functools
import math

import jax
import jax.numpy as jnp
import numpy as np
from jax import lax
from jax.experimental import pallas as pl
from jax.experimental.pallas import tpu as pltpu

EPS = 1e-6
POOL_WINDOWS = (2, 4, 8, 16)
POOL_GROUP = 128
POOL_WIDTH = len(POOL_WINDOWS) * POOL_GROUP
POOL_HALO = 16
RET_HEADS = 6
RET_QK_DIM = 128
RET_V_DIM = 256
RET_CHUNK = 128
SGU_CHUNK = 128
SGU_GROUPS = 8
SGU_GROUP_DIM = 128
SGU_WIDTH = SGU_GROUPS * SGU_GROUP_DIM
DIFF_HEADS = 4
DIFF_HEAD_DIM = 128
DIFF_V_DIM = 2 * DIFF_HEAD_DIM
N_BUCKETS = 32
MAX_DISTANCE = 128
CONV_WIDTH = 3
NEG_INF = -1e30
LOG2E = 1.4426950408889634

V7X_VMEM_BYTES = 64 * 1024 * 1024
VMEM_LIMIT = V7X_VMEM_BYTES - 8 * 1024 * 1024
SUBLANES = 8

F32 = jnp.float32
BF16 = jnp.bfloat16


def _params(*semantics):
    return pltpu.CompilerParams(dimension_semantics=semantics, vmem_limit_bytes=VMEM_LIMIT)


def _rms(x, g):
    return x * lax.rsqrt(jnp.mean(x * x, axis=-1, keepdims=True) + EPS) * g


def _layer_norm(x, g):
    mu = jnp.mean(x, axis=-1, keepdims=True)
    xc = x - mu
    var = jnp.mean(xc * xc, axis=-1, keepdims=True)
    return xc * lax.rsqrt(var + EPS) * g


def _silu(x):
    return x * (1.0 / (1.0 + jnp.exp(-x)))


def _gelu_tanh(x):
    k = -2.0 * math.sqrt(2.0 / math.pi) * LOG2E
    return x / (1.0 + jnp.exp2(x * (k + (k * 0.044715) * (x * x))))


def _weight_rounding_specs(stacks, n_steps, step_of):
    in_specs, out_specs, out_shapes = [], [], []
    for w, layer in stacks:
        _, k, d = w.shape
        rows = k // n_steps
        assert rows * n_steps == k and rows % SUBLANES == 0
        in_specs.append(pl.BlockSpec((None, rows, d), lambda *g, layer=layer: (layer, step_of(*g), 0)))
        out_specs.append(pl.BlockSpec((rows, d), lambda *g: (step_of(*g), 0)))
        out_shapes.append(jax.ShapeDtypeStruct((k, d), BF16))
    return in_specs, out_specs, out_shapes


def _round_weight_slabs(w_refs, w_bf_refs):
    for w_ref, w_bf_ref in zip(w_refs, w_bf_refs):
        w_bf_ref[...] = w_ref[...].astype(BF16)


def _rmsnorm_kernel(x_ref, g_ref, o_ref):
    o_ref[...] = _rms(x_ref[...], g_ref[...]).astype(o_ref.dtype)


def rmsnorm(x, g, *, tm=1024, out_dtype=BF16):
    m, d = x.shape
    return pl.pallas_call(
        _rmsnorm_kernel,
        grid=(m // tm,),
        in_specs=[pl.BlockSpec((tm, d), lambda i: (i, 0)),
                  pl.BlockSpec((1, d), lambda i: (0, 0))],
        out_specs=pl.BlockSpec((tm, d), lambda i: (i, 0)),
        out_shape=jax.ShapeDtypeStruct((m, d), out_dtype),
        compiler_params=_params("parallel"),
        name="rmsnorm",
    )(x, g.reshape(1, d))


def _matmul_kernel(x_ref, w_ref, o_ref, w_bf):
    @pl.when(pl.program_id(1) == 0)
    def _():
        w_bf[...] = w_ref[...].astype(BF16)

    o_ref[...] = jnp.dot(x_ref[...], w_bf[...], preferred_element_type=F32).astype(o_ref.dtype)


def matmul(x, w_stack, layer, *, tm=2048, tn=1024, out_dtype=BF16):
    m, k = x.shape
    n = w_stack.shape[2]
    return pl.pallas_call(
        _matmul_kernel,
        grid=(n // tn, m // tm),
        in_specs=[pl.BlockSpec((tm, k), lambda j, i: (i, 0)),
                  pl.BlockSpec((None, k, tn), lambda j, i: (layer, 0, j))],
        out_specs=pl.BlockSpec((tm, tn), lambda j, i: (i, j)),
        out_shape=jax.ShapeDtypeStruct((m, n), out_dtype),
        scratch_shapes=[pltpu.VMEM((k, tn), BF16)],
        compiler_params=_params("parallel", "arbitrary"),
        name="in_proj",
    )(x, w_stack)


def _proj_res_norm_kernel(*refs, n_pieces, emit_h, tm):
    x_hbm = refs[:n_pieces]
    w_hbm, h_hbm, g_hbm = refs[n_pieces:n_pieces + 3]
    outs_hbm = refs[n_pieces + 3:-2]
    w_vmem, sem = refs[-2:]
    m, d = h_hbm.shape

    copy = pltpu.make_async_copy(w_hbm, w_vmem, sem)
    copy.start()
    copy.wait()

    def body(*tile_refs):
        x_refs = tile_refs[:n_pieces]
        h_ref, g_ref = tile_refs[n_pieces:n_pieces + 2]
        out_refs = tile_refs[n_pieces + 2:]
        if n_pieces > 1:
            x = jnp.concatenate([r[...] for r in x_refs], axis=-1)
        else:
            x = x_refs[0][...]
        h_new = h_ref[...] + jnp.dot(x, w_vmem[...], preferred_element_type=F32)
        if emit_h:
            out_refs[0][...] = h_new
        out_refs[-1][...] = _rms(h_new, g_ref[...]).astype(out_refs[-1].dtype)

    row = lambda width: pl.BlockSpec((tm, width), lambda i: (i, 0))
    pltpu.emit_pipeline(
        body,
        grid=(m // tm,),
        in_specs=[row(r.shape[1]) for r in x_hbm] + [row(d), pl.BlockSpec((1, d), lambda i: (0, 0))],
        out_specs=[row(d)] * len(outs_hbm),
    )(*x_hbm, h_hbm, g_hbm, *outs_hbm)


def proj_res_norm(pieces, w, h, g, *, tm, norm_dtype=BF16, emit_h=True):
    m, d = h.shape
    k = w.shape[0]
    assert sum(p.shape[1] for p in pieces) == k
    any_spec = pl.BlockSpec(memory_space=pl.ANY)
    norm_shape = jax.ShapeDtypeStruct((m, d), norm_dtype)
    if emit_h:
        out_specs, out_shape = [any_spec, any_spec], [jax.ShapeDtypeStruct((m, d), F32), norm_shape]
    else:
        out_specs, out_shape = any_spec, norm_shape
    return pl.pallas_call(
        functools.partial(_proj_res_norm_kernel, n_pieces=len(pieces), emit_h=emit_h, tm=tm),
        in_specs=[any_spec] * (len(pieces) + 3),
        out_specs=out_specs,
        out_shape=out_shape,
        scratch_shapes=[pltpu.VMEM((k, d), w.dtype), pltpu.SemaphoreType.DMA(())],
        compiler_params=pltpu.CompilerParams(vmem_limit_bytes=VMEM_LIMIT),
        name="proj_res_norm",
    )(*pieces, w, h, g.reshape(1, d))


def _shift_rows(a, prev, shift):
    rolled = pltpu.roll(a, shift, 0)
    row = lax.broadcasted_iota(jnp.int32, (SUBLANES, a.shape[1]), 0)
    head = rolled[:SUBLANES]
    for r in range(shift):
        head = jnp.where(row == r, prev[SUBLANES - shift + r:SUBLANES - shift + r + 1], head)
    return jnp.concatenate([head, rolled[SUBLANES:]], axis=0)


def _up_conv_gate_kernel(x_ref, wg_ref, wv_ref, cwg_ref, cwv_ref, cbg_ref, cbv_ref, o_ref,
                         wg_bf, wv_bf, carry, *, tiles_per_seq):
    i = pl.program_id(1)

    @pl.when(i == 0)
    def _():
        wg_bf[...] = wg_ref[...].astype(BF16)
        wv_bf[...] = wv_ref[...].astype(BF16)

    @pl.when(i % tiles_per_seq == 0)
    def _():
        carry[...] = jnp.zeros_like(carry)

    x = x_ref[...]
    tm = x_ref.shape[0]
    gate_bounds = (0, tm // 4, tm)
    value_bounds = (0, 3 * tm // 4, tm)

    def after(w, produced):
        bits = pltpu.bitcast(produced[:2 * SUBLANES, :128], jnp.uint32)
        zero = pltpu.bitcast((bits >> 16) >> 16, F32).astype(BF16)
        top = jnp.concatenate([w[:2 * SUBLANES, :128] + zero, w[:2 * SUBLANES, 128:]], axis=1)
        return jnp.concatenate([top, w[2 * SUBLANES:]], axis=0)

    cwg, cwv = cwg_ref[...], cwv_ref[...]
    wg, wv = wg_bf[...], wv_bf[...]
    prev_g, prev_v = carry[0], carry[1]
    last = None
    activated = []
    for lo, hi in zip(gate_bounds[:-1], gate_bounds[1:]):
        if last is not None:
            wg = after(wg, last)
        last = a_g = jnp.dot(x[lo:hi], wg, preferred_element_type=F32)
        activated.append(_silu(a_g * cwg[2:3] + _shift_rows(a_g, prev_g, 1) * cwg[1:2]
                               + _shift_rows(a_g, prev_g, 2) * cwg[0:1] + cbg_ref[...]))
        prev_g = a_g[hi - lo - SUBLANES:]
    sg = jnp.concatenate(activated, axis=0)
    g2, g1, g0, gb = sg * cwv[2:3], sg * cwv[1:2], sg * cwv[0:1], sg * cbv_ref[...]
    for lo, hi in zip(value_bounds[:-1], value_bounds[1:]):
        wv = after(wv, last)
        last = a_v = jnp.dot(x[lo:hi], wv, preferred_element_type=F32)
        o_ref[lo:hi, :] = (a_v * g2[lo:hi] + _shift_rows(a_v, prev_v, 1) * g1[lo:hi]
                           + _shift_rows(a_v, prev_v, 2) * g0[lo:hi] + gb[lo:hi]).astype(o_ref.dtype)
        prev_v = a_v[hi - lo - SUBLANES:]
    carry[0] = prev_g
    carry[1] = prev_v


def up_conv_gate(x, w_up_stack, layer, conv_w, conv_b, *, seq, tm=1024, tn=512):
    m, k = x.shape
    d_ff = w_up_stack.shape[2] // 2
    nj = d_ff // tn
    conv_b = conv_b.reshape(1, 2 * d_ff)
    return pl.pallas_call(
        functools.partial(_up_conv_gate_kernel, tiles_per_seq=seq // tm),
        grid=(nj, m // tm),
        in_specs=[pl.BlockSpec((tm, k), lambda j, i: (i, 0)),
                  pl.BlockSpec((None, k, tn), lambda j, i: (layer, 0, j)),
                  pl.BlockSpec((None, k, tn), lambda j, i: (layer, 0, nj + j)),
                  pl.BlockSpec((CONV_WIDTH, tn), lambda j, i: (0, j)),
                  pl.BlockSpec((CONV_WIDTH, tn), lambda j, i: (0, nj + j)),
                  pl.BlockSpec((1, tn), lambda j, i: (0, j)),
                  pl.BlockSpec((1, tn), lambda j, i: (0, nj + j))],
        out_specs=pl.BlockSpec((tm, tn), lambda j, i: (i, j)),
        out_shape=jax.ShapeDtypeStruct((m, d_ff), BF16),
        scratch_shapes=[pltpu.VMEM((k, tn), BF16), pltpu.VMEM((k, tn), BF16),
                        pltpu.VMEM((2, SUBLANES, tn), F32)],
        compiler_params=_params("parallel", "arbitrary"),
        name="up_conv_gate",
    )(x, w_up_stack, w_up_stack, conv_w, conv_w, conv_b, conv_b)


def _pool_retention_kernel(cd_ref, z_ref, cos_ref, sin_ref, intra_ref, qdec_ref, kdec_ref, poolw_ref,
                           pscale_ref, gn_ref, *rest, tile, n_weights):
    w_refs, (a_ref, b_ref), w_bf_refs = rest[:n_weights], rest[n_weights:n_weights + 2], rest[n_weights + 2:-2]
    u_carry, state = rest[-2:]
    _round_weight_slabs(w_refs, w_bf_refs)
    t = pl.program_id(1)

    @pl.when(t == 0)
    def _():
        u_carry[...] = jnp.zeros_like(u_carry)
        state[...] = jnp.zeros_like(state)

    u = z_ref[:, :POOL_WIDTH].astype(F32)
    ext = jnp.concatenate([u_carry[...], u], axis=0)
    u_carry[...] = u[tile - POOL_HALO:]
    pos = (t * tile + 1 + lax.broadcasted_iota(jnp.int32, (tile, POOL_GROUP), 0)).astype(F32)
    for gi, w in enumerate(POOL_WINDOWS):
        cols = slice(gi * POOL_GROUP, (gi + 1) * POOL_GROUP)
        s = ext[:, cols]
        span = 1
        while span < w:
            s = s + pltpu.roll(s, span, 0)
            span *= 2
        pooled = s[POOL_HALO:] / jnp.minimum(pos, float(w))
        y = (pooled - u[:, cols]).astype(BF16)
        a = jnp.dot(y, poolw_ref[gi], preferred_element_type=F32)
        a_ref[:, cols] = (a * pscale_ref[:, cols]).astype(a_ref.dtype)

    q0 = POOL_WIDTH
    k0 = q0 + RET_HEADS * RET_QK_DIM
    v0 = k0 + RET_HEADS * RET_QK_DIM
    g0 = v0 + RET_HEADS * RET_V_DIM
    half = RET_QK_DIM // 2

    def chunk_body(c, carry):
        r0 = pl.multiple_of(c * RET_CHUNK, RET_CHUNK)
        rows = pl.ds(r0, RET_CHUNK)
        cos = cos_ref[rows, :]
        sin = sin_ref[rows, :]
        for h in range(RET_HEADS):
            q = z_ref[rows, q0 + h * RET_QK_DIM:q0 + (h + 1) * RET_QK_DIM].astype(F32)
            k = z_ref[rows, k0 + h * RET_QK_DIM:k0 + (h + 1) * RET_QK_DIM].astype(F32)
            v = z_ref[rows, v0 + h * RET_V_DIM:v0 + (h + 1) * RET_V_DIM]
            gate = z_ref[rows, g0 + h * RET_V_DIM:g0 + (h + 1) * RET_V_DIM].astype(F32)
            qr = q * cos + pltpu.roll(q, half, 1) * sin
            kr = (k * cos + pltpu.roll(k, half, 1) * sin) * (RET_QK_DIM ** -0.5)
            scores = lax.dot_general(qr.astype(BF16), kr.astype(BF16), (((1,), (1,)), ((), ())),
                                     preferred_element_type=F32) * intra_ref[h]
            inner = jnp.dot(scores.astype(BF16), v, preferred_element_type=F32)
            st = state[h]
            cross = jnp.dot((qr * qdec_ref[h]).astype(BF16), st.astype(BF16), preferred_element_type=F32)
            kv = lax.dot_general((kr * kdec_ref[h]).astype(BF16), v, (((0,), (0,)), ((), ())),
                                 preferred_element_type=F32)
            state[h] = st * cd_ref[h] + kv
            vcols = slice(h * RET_V_DIM, (h + 1) * RET_V_DIM)
            r = _layer_norm(inner + cross, gn_ref[:, vcols])
            b_ref[rows, vcols] = (r * _silu(gate)).astype(b_ref.dtype)
        return carry

    lax.fori_loop(0, tile // RET_CHUNK, chunk_body, 0, unroll=2)


def _retention_tables(seq):
    f32 = np.float32
    half = RET_QK_DIM // 2
    inv = (1.0 / (10000.0 ** (np.arange(half, dtype=f32) / f32(half)))).astype(f32)
    ang = np.arange(seq, dtype=f32)[:, None] * inv[None, :]
    cos, sin = np.cos(ang), np.sin(ang)
    cos_full = np.concatenate([cos, cos], axis=-1)
    sin_signed = np.concatenate([-sin, sin], axis=-1)
    c = RET_CHUNK
    log_g = np.log(1.0 - 2.0 ** (-5.0 - np.arange(RET_HEADS, dtype=f32))).astype(f32)
    idx = np.arange(c, dtype=f32)
    diff = idx[:, None] - idx[None, :]
    intra = np.where(diff >= 0, np.exp(log_g[:, None, None] * np.maximum(diff, 0.0)), 0.0).astype(f32)
    q_dec = np.exp(log_g[:, None] * (idx[None, :] + 1.0)).astype(f32)
    k_dec = np.exp(log_g[:, None] * (c - 1.0 - idx[None, :])).astype(f32)
    chunk_dec = np.exp(log_g * c).astype(f32)
    q_dec = np.ascontiguousarray(np.broadcast_to(q_dec[:, :, None], (RET_HEADS, c, RET_QK_DIM)))
    k_dec = np.ascontiguousarray(np.broadcast_to(k_dec[:, :, None], (RET_HEADS, c, RET_QK_DIM)))
    return cos_full, sin_signed, intra, q_dec, k_dec, chunk_dec


def pool_retention(z, pool_w, pool_scale, ret_gn_g, round_weights, *, batch, seq, tile=512):
    m, width = z.shape
    nt = seq // tile
    w_in_specs, w_out_specs, w_out_shapes = _weight_rounding_specs(round_weights, batch * nt, lambda b, t: b * nt + t)
    cos, sin, intra, q_dec, k_dec, chunk_dec = _retention_tables(seq)
    vw = RET_HEADS * RET_V_DIM
    const3 = lambda b, t: (0, 0, 0)
    return pl.pallas_call(
        functools.partial(_pool_retention_kernel, tile=tile, n_weights=len(round_weights)),
        grid=(batch, nt),
        in_specs=[pl.BlockSpec(memory_space=pltpu.SMEM),
                  pl.BlockSpec((tile, width), lambda b, t: (b * nt + t, 0)),
                  pl.BlockSpec((tile, RET_QK_DIM), lambda b, t: (t, 0)),
                  pl.BlockSpec((tile, RET_QK_DIM), lambda b, t: (t, 0)),
                  pl.BlockSpec(intra.shape, const3),
                  pl.BlockSpec(q_dec.shape, const3),
                  pl.BlockSpec(k_dec.shape, const3),
                  pl.BlockSpec(pool_w.shape, const3),
                  pl.BlockSpec((1, POOL_WIDTH), lambda b, t: (0, 0)),
                  pl.BlockSpec((1, vw), lambda b, t: (0, 0))] + w_in_specs,
        out_specs=[pl.BlockSpec((tile, POOL_WIDTH), lambda b, t: (b * nt + t, 0)),
                   pl.BlockSpec((tile, vw), lambda b, t: (b * nt + t, 0))] + w_out_specs,
        out_shape=[jax.ShapeDtypeStruct((m, POOL_WIDTH), BF16), jax.ShapeDtypeStruct((m, vw), BF16)] + w_out_shapes,
        scratch_shapes=[pltpu.VMEM((POOL_HALO, POOL_WIDTH), F32),
                        pltpu.VMEM((RET_HEADS, RET_QK_DIM, RET_V_DIM), F32)],
        compiler_params=_params("parallel", "arbitrary"),
        name="pool_retention",
    )(chunk_dec, z, cos, sin, intra, q_dec, k_dec, pool_w.astype(BF16),
      pool_scale.reshape(1, POOL_WIDTH), ret_gn_g.reshape(1, vw), *[w for w, _ in round_weights])


def _sgu_kernel(z_ref, lng_ref, ws_ref, bs_ref, o_ref, *, tile):
    zc = _gelu_tanh(z_ref[...].astype(F32))
    zu = zc[:, :SGU_WIDTH]
    v = _layer_norm(zc[:, SGU_WIDTH:], lng_ref[...]).astype(BF16)
    row = lax.broadcasted_iota(jnp.int32, (SGU_CHUNK, SGU_CHUNK), 0)
    col = lax.broadcasted_iota(jnp.int32, (SGU_CHUNK, SGU_CHUNK), 1)
    for g in range(SGU_GROUPS):
        wm = jnp.where(row >= col, ws_ref[g], 0.0).astype(BF16)
        bias = bs_ref[g]
        cols = slice(g * SGU_GROUP_DIM, (g + 1) * SGU_GROUP_DIM)
        for c in range(tile // SGU_CHUNK):
            rows = slice(c * SGU_CHUNK, (c + 1) * SGU_CHUNK)
            sv = jnp.dot(wm, v[rows, cols], preferred_element_type=F32) + bias
            o_ref[rows, cols] = (zu[rows, cols] * sv).astype(o_ref.dtype)


def spatial_gating(z, ln_g, w_s, b_s, *, tile=1024):
    m = z.shape[0]
    return pl.pallas_call(
        functools.partial(_sgu_kernel, tile=tile),
        grid=(m // tile,),
        in_specs=[pl.BlockSpec((tile, 2 * SGU_WIDTH), lambda i: (i, 0)),
                  pl.BlockSpec((1, SGU_WIDTH), lambda i: (0, 0)),
                  pl.BlockSpec(w_s.shape, lambda i: (0, 0, 0)),
                  pl.BlockSpec((SGU_GROUPS, SGU_CHUNK, 1), lambda i: (0, 0, 0))],
        out_specs=pl.BlockSpec((tile, SGU_WIDTH), lambda i: (i, 0)),
        out_shape=jax.ShapeDtypeStruct((m, SGU_WIDTH), BF16),
        compiler_params=_params("parallel"),
        name="spatial_gating",
    )(z, ln_g.reshape(1, SGU_WIDTH), w_s, b_s.reshape(SGU_GROUPS, SGU_CHUNK, 1))


def _t5_bucket(rel):
    n = np.maximum(rel, 0)
    max_exact = N_BUCKETS // 2
    large = max_exact + (np.log(np.maximum(n, 1).astype(np.float32) / np.float32(max_exact))
                         / np.float32(math.log(MAX_DISTANCE / max_exact))
                         * np.float32(N_BUCKETS - max_exact)).astype(np.int32)
    large = np.minimum(large, N_BUCKETS - 1)
    return np.where(n < max_exact, n, large).astype(np.int32)


def _bias_tiles_kernel(relb_ref, bucket_ref, o_ref):
    h = pl.program_id(0)
    bucket = bucket_ref[...]
    acc = jnp.zeros(bucket.shape, F32)
    for b in range(N_BUCKETS):
        acc = jnp.where(bucket == b, relb_ref[b, h], acc)
    far_bias = relb_ref[N_BUCKETS - 1, h]
    o_ref[0] = (acc - far_bias) * LOG2E


def rel_bias_tiles(rel_bias, *, tile):
    qk = np.arange(tile)[:, None] - np.arange(tile)[None, :]
    bucket = np.stack([_t5_bucket(qk), _t5_bucket(tile + qk)])
    return pl.pallas_call(
        _bias_tiles_kernel,
        grid=(DIFF_HEADS,),
        in_specs=[pl.BlockSpec(memory_space=pltpu.SMEM),
                  pl.BlockSpec((2, tile, tile), lambda h: (0, 0, 0))],
        out_specs=pl.BlockSpec((1, 2, tile, tile), lambda h: (h, 0, 0, 0)),
        out_shape=jax.ShapeDtypeStruct((DIFF_HEADS, 2, tile, tile), F32),
        compiler_params=_params("parallel"),
        name="rel_bias_tiles",
    )(rel_bias, bucket)


def _diff_attn_kernel(lq1_ref, lk1_ref, lq2_ref, lk2_ref, q_ref, k_ref, v_ref, bias_ref, subg_ref,
                      *rest, tile, seq, lam_init, n_weights):
    w_refs, o_ref, w_bf_refs = rest[:n_weights], rest[n_weights], rest[n_weights + 1:]
    _round_weight_slabs(w_refs, w_bf_refs)
    dh = DIFF_HEAD_DIM
    lam = (jnp.exp(jnp.sum(lq1_ref[...] * lk1_ref[...], axis=-1, keepdims=True))
           - jnp.exp(jnp.sum(lq2_ref[...] * lk2_ref[...], axis=-1, keepdims=True)) + lam_init)
    row = lax.broadcasted_iota(jnp.int32, (tile, tile), 0)
    col = lax.broadcasted_iota(jnp.int32, (tile, tile), 1)
    causal = row >= col

    def query_tile(qi):
        kv_len = (qi + 1) * tile
        rows = slice(qi * tile, kv_len)
        q = (q_ref[rows, :].astype(F32) * (dh ** -0.5 * LOG2E)).astype(BF16)
        streams = []
        for s in range(2):
            sc = lax.dot_general(q[:, s * dh:(s + 1) * dh], k_ref[:kv_len, s * dh:(s + 1) * dh],
                                 (((1,), (1,)), ((), ())), preferred_element_type=F32)
            parts = []
            if qi >= 2:
                parts.append(sc[:, :kv_len - 2 * tile])
            if qi >= 1:
                parts.append(sc[:, kv_len - 2 * tile:kv_len - tile] + bias_ref[0, 1])
            parts.append(jnp.where(causal, sc[:, kv_len - tile:] + bias_ref[0, 0], NEG_INF))
            sc = jnp.concatenate(parts, axis=1) if len(parts) > 1 else parts[0]
            p = jnp.exp2(sc - jnp.max(sc, axis=-1, keepdims=True))
            l = jnp.sum(p, axis=-1, keepdims=True)
            streams.append(jnp.dot(p.astype(BF16), v_ref[:kv_len, :], preferred_element_type=F32) / l)
        d = streams[0] - lam * streams[1]
        o_ref[rows, :] = (_rms(d, subg_ref[...]) * (1.0 - lam_init)).astype(o_ref.dtype)

    n_tiles = seq // tile
    always = pl.program_id(0) >= 0
    for first in range(n_tiles // 2):
        @pl.when(always)
        def _(first=first):
            query_tile(first)
            query_tile(n_tiles - 1 - first)


def diff_attention(z, bias_tiles, lq1, lk1, lq2, lk2, subln_g, round_weights, *, batch, seq, layer_idx,
                   tile):
    m = z.shape[0]
    assert tile >= MAX_DISTANCE and seq % (2 * tile) == 0
    w_in_specs, w_out_specs, w_out_shapes = _weight_rounding_specs(
        round_weights, batch * DIFF_HEADS, lambda b, h: b * DIFF_HEADS + h)
    w = 2 * DIFF_HEAD_DIM
    q_blk = 2 * SGU_WIDTH // w
    k_blk = q_blk + DIFF_HEADS
    v_blk = k_blk + DIFF_HEADS
    lam_init = 0.8 - 0.6 * math.exp(-0.3 * layer_idx)
    vec = lambda a: a.reshape(1, DIFF_HEAD_DIM)
    vec_spec = pl.BlockSpec((1, DIFF_HEAD_DIM), lambda b, h: (0, 0))
    return pl.pallas_call(
        functools.partial(_diff_attn_kernel, tile=tile, seq=seq, lam_init=lam_init, n_weights=len(round_weights)),
        grid=(batch, DIFF_HEADS),
        in_specs=[vec_spec, vec_spec, vec_spec, vec_spec,
                  pl.BlockSpec((seq, w), lambda b, h: (b, q_blk + h)),
                  pl.BlockSpec((seq, w), lambda b, h: (b, k_blk + h)),
                  pl.BlockSpec((seq, w), lambda b, h: (b, v_blk + h)),
                  pl.BlockSpec((1, 2, tile, tile), lambda b, h: (h, 0, 0, 0)),
                  pl.BlockSpec((1, DIFF_V_DIM), lambda b, h: (0, 0))] + w_in_specs,
        out_specs=[pl.BlockSpec((seq, DIFF_V_DIM), lambda b, h: (b, h))] + w_out_specs,
        out_shape=[jax.ShapeDtypeStruct((m, DIFF_HEADS * DIFF_V_DIM), BF16)] + w_out_shapes,
        compiler_params=_params("parallel", "parallel"),
        name="diff_attention",
    )(vec(lq1), vec(lk1), vec(lq2), vec(lk2), z, z, z, bias_tiles, subln_g.reshape(1, DIFF_V_DIM),
      *[w for w, _ in round_weights])


ATTN_TILE = 256
ROW_TILE_OUT = 512
ROW_TILE_DOWN = 512


def kernel(x, w_in_even, w_out_even, pool_w, pool_scale, ret_gn_g, w_in_odd, w_out_odd, sgu_ln_g, sgu_w, sgu_b,
           lam_q1, lam_k1, lam_q2, lam_k2, diff_subln_g, rel_bias, mix_norm_g, ffn_norm_g, w_up, conv_w, conv_b,
           w_down, final_norm_g):
    batch, seq, d = x.shape
    depth = mix_norm_g.shape[0]
    h = x.reshape(batch * seq, d)
    hn = rmsnorm(h, mix_norm_g[0])
    bias_tiles = rel_bias_tiles(rel_bias, tile=ATTN_TILE)
    out = None
    for i in range(depth):
        if i % 2 == 0:
            e = i // 2
            z = matmul(hn, w_in_even, e)
            a_out, b_out, w_out, w_dn = pool_retention(z, pool_w[e], pool_scale[e], ret_gn_g[e],
                                                        [(w_out_even, e), (w_down, i)], batch=batch, seq=seq)
            pieces = [a_out, b_out]
        else:
            o = i // 2
            z = matmul(hn, w_in_odd, o)
            c_out = spatial_gating(z, sgu_ln_g[o], sgu_w[o], sgu_b[o])
            d_out, w_out, w_dn = diff_attention(z, bias_tiles, lam_q1[o], lam_k1[o], lam_q2[o], lam_k2[o],
                                                diff_subln_g[o], [(w_out_odd, o), (w_down, i)],
                                                batch=batch, seq=seq, layer_idx=i, tile=ATTN_TILE)
            pieces = [c_out, d_out]
        h, hn = proj_res_norm(pieces, w_out, h, ffn_norm_g[i], tm=ROW_TILE_OUT)
        act = up_conv_gate(hn, w_up, i, conv_w[i], conv_b[i], seq=seq)
        if i + 1 < depth:
            h, hn = proj_res_norm([act], w_dn, h, mix_norm_g[i + 1], tm=ROW_TILE_DOWN)
        else:
            out = proj_res_norm([act], w_dn, h, final_norm_g, tm=ROW_TILE_DOWN,
                                norm_dtype=x.dtype, emit_h=False)
    return out.reshape(batch, seq, d)
```

```python
import functools
import math

import jax
import jax.numpy as jnp
import numpy as np
from jax import lax
from jax.experimental import pallas as pl
from jax.experimental.pallas import tpu as pltpu

EPS = 1e-6
POOL_WINDOWS = (2, 4, 8, 16)
POOL_GROUP = 128
POOL_WIDTH = len(POOL_WINDOWS) * POOL_GROUP
POOL_HALO = 16
RET_HEADS = 6
RET_QK_DIM = 128
RET_V_DIM = 256
RET_CHUNK = 128
SGU_CHUNK = 128
SGU_GROUPS = 8
SGU_GROUP_DIM = 128
SGU_WIDTH = SGU_GROUPS * SGU_GROUP_DIM
DIFF_HEADS = 4
DIFF_HEAD_DIM = 128
DIFF_V_DIM = 2 * DIFF_HEAD_DIM
N_BUCKETS = 32
MAX_DISTANCE = 128
CONV_WIDTH = 3
NEG_INF = -1e30
LOG2E = 1.4426950408889634

V7X_VMEM_BYTES = 64 * 1024 * 1024
VMEM_LIMIT = V7X_VMEM_BYTES - 8 * 1024 * 1024
SUBLANES = 8

F32 = jnp.float32
BF16 = jnp.bfloat16


def _params(*semantics):
    return pltpu.CompilerParams(dimension_semantics=semantics, vmem_limit_bytes=VMEM_LIMIT)


def _rms(x, g):
    return x * lax.rsqrt(jnp.mean(x * x, axis=-1, keepdims=True) + EPS) * g


def _layer_norm(x, g):
    mu = jnp.mean(x, axis=-1, keepdims=True)
    xc = x - mu
    var = jnp.mean(xc * xc, axis=-1, keepdims=True)
    return xc * lax.rsqrt(var + EPS) * g


def _silu(x):
    return x * (1.0 / (1.0 + jnp.exp(-x)))


def _gelu_tanh(x):
    k = -2.0 * math.sqrt(2.0 / math.pi) * LOG2E
    return x / (1.0 + jnp.exp2(x * (k + (k * 0.044715) * (x * x))))


def _weight_rounding_specs(stacks, n_steps, step_of):
    in_specs, out_specs, out_shapes = [], [], []
    for w, layer in stacks:
        _, k, d = w.shape
        rows = k // n_steps
        assert rows * n_steps == k and rows % SUBLANES == 0
        in_specs.append(pl.BlockSpec((None, rows, d), lambda *g, layer=layer: (layer, step_of(*g), 0)))
        out_specs.append(pl.BlockSpec((rows, d), lambda *g: (step_of(*g), 0)))
        out_shapes.append(jax.ShapeDtypeStruct((k, d), BF16))
    return in_specs, out_specs, out_shapes


def _round_weight_slabs(w_refs, w_bf_refs):
    for w_ref, w_bf_ref in zip(w_refs, w_bf_refs):
        w_bf_ref[...] = w_ref[...].astype(BF16)


def _rmsnorm_kernel(x_ref, g_ref, o_ref):
    o_ref[...] = _rms(x_ref[...], g_ref[...]).astype(o_ref.dtype)


def rmsnorm(x, g, *, tm=1024, out_dtype=BF16):
    m, d = x.shape
    return pl.pallas_call(
        _rmsnorm_kernel,
        grid=(m // tm,),
        in_specs=[pl.BlockSpec((tm, d), lambda i: (i, 0)),
                  pl.BlockSpec((1, d), lambda i: (0, 0))],
        out_specs=pl.BlockSpec((tm, d), lambda i: (i, 0)),
        out_shape=jax.ShapeDtypeStruct((m, d), out_dtype),
        compiler_params=_params("parallel"),
        name="rmsnorm",
    )(x, g.reshape(1, d))


def _matmul_kernel(x_ref, w_ref, o_ref, w_bf):
    @pl.when(pl.program_id(1) == 0)
    def _():
        w_bf[...] = w_ref[...].astype(BF16)

    o_ref[...] = jnp.dot(x_ref[...], w_bf[...], preferred_element_type=F32).astype(o_ref.dtype)


def matmul(x, w_stack, layer, *, tm=2048, tn=1024, out_dtype=BF16):
    m, k = x.shape
    n = w_stack.shape[2]
    return pl.pallas_call(
        _matmul_kernel,
        grid=(n // tn, m // tm),
        in_specs=[pl.BlockSpec((tm, k), lambda j, i: (i, 0)),
                  pl.BlockSpec((None, k, tn), lambda j, i: (layer, 0, j))],
        out_specs=pl.BlockSpec((tm, tn), lambda j, i: (i, j)),
        out_shape=jax.ShapeDtypeStruct((m, n), out_dtype),
        scratch_shapes=[pltpu.VMEM((k, tn), BF16)],
        compiler_params=_params("parallel", "arbitrary"),
        name="in_proj",
    )(x, w_stack)


def _proj_res_norm_kernel(*refs, n_pieces, emit_h):
    x_refs = refs[:n_pieces]
    w_ref, h_ref, g_ref = refs[n_pieces:n_pieces + 3]
    out_refs = refs[n_pieces + 3:]
    if n_pieces > 1:
        x = jnp.concatenate([r[...] for r in x_refs], axis=-1)
    else:
        x = x_refs[0][...]
    h_new = h_ref[...] + jnp.dot(x, w_ref[...], preferred_element_type=F32)
    if emit_h:
        out_refs[0][...] = h_new
    out_refs[-1][...] = _rms(h_new, g_ref[...]).astype(out_refs[-1].dtype)


def proj_res_norm(pieces, w, h, g, *, tm, norm_dtype=BF16, emit_h=True):
    m, d = h.shape
    k = w.shape[0]
    assert sum(p.shape[1] for p in pieces) == k
    in_specs = [pl.BlockSpec((tm, p.shape[1]), lambda i: (i, 0)) for p in pieces]
    in_specs += [pl.BlockSpec((k, d), lambda i: (0, 0), pipeline_mode=pl.Buffered(1)),
                 pl.BlockSpec((tm, d), lambda i: (i, 0)),
                 pl.BlockSpec((1, d), lambda i: (0, 0))]
    row_spec = pl.BlockSpec((tm, d), lambda i: (i, 0))
    norm_shape = jax.ShapeDtypeStruct((m, d), norm_dtype)
    if emit_h:
        out_specs, out_shape = [row_spec, row_spec], [jax.ShapeDtypeStruct((m, d), F32), norm_shape]
    else:
        out_specs, out_shape = row_spec, norm_shape
    return pl.pallas_call(
        functools.partial(_proj_res_norm_kernel, n_pieces=len(pieces), emit_h=emit_h),
        grid=(m // tm,),
        in_specs=in_specs,
        out_specs=out_specs,
        out_shape=out_shape,
        compiler_params=_params("parallel"),
        name="proj_res_norm",
    )(*pieces, w, h, g.reshape(1, d))


def _shift_rows(a, prev, shift):
    rolled = pltpu.roll(a, shift, 0)
    row = lax.broadcasted_iota(jnp.int32, (SUBLANES, a.shape[1]), 0)
    head = rolled[:SUBLANES]
    for r in range(shift):
        head = jnp.where(row == r, prev[SUBLANES - shift + r:SUBLANES - shift + r + 1], head)
    return jnp.concatenate([head, rolled[SUBLANES:]], axis=0)


def _up_conv_gate_kernel(x_ref, wg_ref, wv_ref, cwg_ref, cwv_ref, cbg_ref, cbv_ref, o_ref,
                         wg_bf, wv_bf, carry, *, tiles_per_seq):
    i = pl.program_id(1)

    @pl.when(i == 0)
    def _():
        wg_bf[...] = wg_ref[...].astype(BF16)
        wv_bf[...] = wv_ref[...].astype(BF16)

    @pl.when(i % tiles_per_seq == 0)
    def _():
        carry[...] = jnp.zeros_like(carry)

    x = x_ref[...]
    tm = x_ref.shape[0]
    gate_bounds = (0, tm // 4, tm)
    value_bounds = (0, 3 * tm // 4, tm)

    def after(w, produced):
        bits = pltpu.bitcast(produced[:2 * SUBLANES, :128], jnp.uint32)
        zero = pltpu.bitcast((bits >> 16) >> 16, F32).astype(BF16)
        top = jnp.concatenate([w[:2 * SUBLANES, :128] + zero, w[:2 * SUBLANES, 128:]], axis=1)
        return jnp.concatenate([top, w[2 * SUBLANES:]], axis=0)

    cwg, cwv = cwg_ref[...], cwv_ref[...]
    wg, wv = wg_bf[...], wv_bf[...]
    prev_g, prev_v = carry[0], carry[1]
    last = None
    activated = []
    for lo, hi in zip(gate_bounds[:-1], gate_bounds[1:]):
        if last is not None:
            wg = after(wg, last)
        last = a_g = jnp.dot(x[lo:hi], wg, preferred_element_type=F32)
        activated.append(_silu(a_g * cwg[2:3] + _shift_rows(a_g, prev_g, 1) * cwg[1:2]
                               + _shift_rows(a_g, prev_g, 2) * cwg[0:1] + cbg_ref[...]))
        prev_g = a_g[hi - lo - SUBLANES:]
    sg = jnp.concatenate(activated, axis=0)
    g2, g1, g0, gb = sg * cwv[2:3], sg * cwv[1:2], sg * cwv[0:1], sg * cbv_ref[...]
    for lo, hi in zip(value_bounds[:-1], value_bounds[1:]):
        wv = after(wv, last)
        last = a_v = jnp.dot(x[lo:hi], wv, preferred_element_type=F32)
        o_ref[lo:hi, :] = (a_v * g2[lo:hi] + _shift_rows(a_v, prev_v, 1) * g1[lo:hi]
                           + _shift_rows(a_v, prev_v, 2) * g0[lo:hi] + gb[lo:hi]).astype(o_ref.dtype)
        prev_v = a_v[hi - lo - SUBLANES:]
    carry[0] = prev_g
    carry[1] = prev_v


def up_conv_gate(x, w_up_stack, layer, conv_w, conv_b, *, seq, tm=1024, tn=512):
    m, k = x.shape
    d_ff = w_up_stack.shape[2] // 2
    nj = d_ff // tn
    conv_b = conv_b.reshape(1, 2 * d_ff)
    return pl.pallas_call(
        functools.partial(_up_conv_gate_kernel, tiles_per_seq=seq // tm),
        grid=(nj, m // tm),
        in_specs=[pl.BlockSpec((tm, k), lambda j, i: (i, 0)),
                  pl.BlockSpec((None, k, tn), lambda j, i: (layer, 0, j)),
                  pl.BlockSpec((None, k, tn), lambda j, i: (layer, 0, nj + j)),
                  pl.BlockSpec((CONV_WIDTH, tn), lambda j, i: (0, j)),
                  pl.BlockSpec((CONV_WIDTH, tn), lambda j, i: (0, nj + j)),
                  pl.BlockSpec((1, tn), lambda j, i: (0, j)),
                  pl.BlockSpec((1, tn), lambda j, i: (0, nj + j))],
        out_specs=pl.BlockSpec((tm, tn), lambda j, i: (i, j)),
        out_shape=jax.ShapeDtypeStruct((m, d_ff), BF16),
        scratch_shapes=[pltpu.VMEM((k, tn), BF16), pltpu.VMEM((k, tn), BF16),
                        pltpu.VMEM((2, SUBLANES, tn), F32)],
        compiler_params=_params("parallel", "arbitrary"),
        name="up_conv_gate",
    )(x, w_up_stack, w_up_stack, conv_w, conv_w, conv_b, conv_b)


def _pool_retention_kernel(cd_ref, z_ref, cos_ref, sin_ref, intra_ref, qdec_ref, kdec_ref, poolw_ref,
                           pscale_ref, gn_ref, *rest, tile, n_weights):
    w_refs, (a_ref, b_ref), w_bf_refs = rest[:n_weights], rest[n_weights:n_weights + 2], rest[n_weights + 2:-2]
    u_carry, state = rest[-2:]
    _round_weight_slabs(w_refs, w_bf_refs)
    t = pl.program_id(1)

    @pl.when(t == 0)
    def _():
        u_carry[...] = jnp.zeros_like(u_carry)
        state[...] = jnp.zeros_like(state)

    u = z_ref[:, :POOL_WIDTH].astype(F32)
    ext = jnp.concatenate([u_carry[...], u], axis=0)
    u_carry[...] = u[tile - POOL_HALO:]
    pos = (t * tile + 1 + lax.broadcasted_iota(jnp.int32, (tile, POOL_GROUP), 0)).astype(F32)
    for gi, w in enumerate(POOL_WINDOWS):
        cols = slice(gi * POOL_GROUP, (gi + 1) * POOL_GROUP)
        s = ext[:, cols]
        span = 1
        while span < w:
            s = s + pltpu.roll(s, span, 0)
            span *= 2
        pooled = s[POOL_HALO:] / jnp.minimum(pos, float(w))
        y = (pooled - u[:, cols]).astype(BF16)
        a = jnp.dot(y, poolw_ref[gi], preferred_element_type=F32)
        a_ref[:, cols] = (a * pscale_ref[:, cols]).astype(a_ref.dtype)

    q0 = POOL_WIDTH
    k0 = q0 + RET_HEADS * RET_QK_DIM
    v0 = k0 + RET_HEADS * RET_QK_DIM
    g0 = v0 + RET_HEADS * RET_V_DIM
    half = RET_QK_DIM // 2

    def chunk_body(c, carry):
        r0 = pl.multiple_of(c * RET_CHUNK, RET_CHUNK)
        rows = pl.ds(r0, RET_CHUNK)
        cos = cos_ref[rows, :]
        sin = sin_ref[rows, :]
        for h in range(RET_HEADS):
            q = z_ref[rows, q0 + h * RET_QK_DIM:q0 + (h + 1) * RET_QK_DIM].astype(F32)
            k = z_ref[rows, k0 + h * RET_QK_DIM:k0 + (h + 1) * RET_QK_DIM].astype(F32)
            v = z_ref[rows, v0 + h * RET_V_DIM:v0 + (h + 1) * RET_V_DIM]
            gate = z_ref[rows, g0 + h * RET_V_DIM:g0 + (h + 1) * RET_V_DIM].astype(F32)
            qr = q * cos + pltpu.roll(q, half, 1) * sin
            kr = (k * cos + pltpu.roll(k, half, 1) * sin) * (RET_QK_DIM ** -0.5)
            scores = lax.dot_general(qr.astype(BF16), kr.astype(BF16), (((1,), (1,)), ((), ())),
                                     preferred_element_type=F32) * intra_ref[h]
            inner = jnp.dot(scores.astype(BF16), v, preferred_element_type=F32)
            st = state[h]
            cross = jnp.dot((qr * qdec_ref[h]).astype(BF16), st.astype(BF16), preferred_element_type=F32)
            kv = lax.dot_general((kr * kdec_ref[h]).astype(BF16), v, (((0,), (0,)), ((), ())),
                                 preferred_element_type=F32)
            state[h] = st * cd_ref[h] + kv
            vcols = slice(h * RET_V_DIM, (h + 1) * RET_V_DIM)
            r = _layer_norm(inner + cross, gn_ref[:, vcols])
            b_ref[rows, vcols] = (r * _silu(gate)).astype(b_ref.dtype)
        return carry

    lax.fori_loop(0, tile // RET_CHUNK, chunk_body, 0, unroll=2)


def _retention_tables(seq):
    f32 = np.float32
    half = RET_QK_DIM // 2
    inv = (1.0 / (10000.0 ** (np.arange(half, dtype=f32) / f32(half)))).astype(f32)
    ang = np.arange(seq, dtype=f32)[:, None] * inv[None, :]
    cos, sin = np.cos(ang), np.sin(ang)
    cos_full = np.concatenate([cos, cos], axis=-1)
    sin_signed = np.concatenate([-sin, sin], axis=-1)
    c = RET_CHUNK
    log_g = np.log(1.0 - 2.0 ** (-5.0 - np.arange(RET_HEADS, dtype=f32))).astype(f32)
    idx = np.arange(c, dtype=f32)
    diff = idx[:, None] - idx[None, :]
    intra = np.where(diff >= 0, np.exp(log_g[:, None, None] * np.maximum(diff, 0.0)), 0.0).astype(f32)
    q_dec = np.exp(log_g[:, None] * (idx[None, :] + 1.0)).astype(f32)
    k_dec = np.exp(log_g[:, None] * (c - 1.0 - idx[None, :])).astype(f32)
    chunk_dec = np.exp(log_g * c).astype(f32)
    q_dec = np.ascontiguousarray(np.broadcast_to(q_dec[:, :, None], (RET_HEADS, c, RET_QK_DIM)))
    k_dec = np.ascontiguousarray(np.broadcast_to(k_dec[:, :, None], (RET_HEADS, c, RET_QK_DIM)))
    return cos_full, sin_signed, intra, q_dec, k_dec, chunk_dec


def pool_retention(z, pool_w, pool_scale, ret_gn_g, round_weights, *, batch, seq, tile=512):
    m, width = z.shape
    nt = seq // tile
    w_in_specs, w_out_specs, w_out_shapes = _weight_rounding_specs(round_weights, batch * nt, lambda b, t: b * nt + t)
    cos, sin, intra, q_dec, k_dec, chunk_dec = _retention_tables(seq)
    vw = RET_HEADS * RET_V_DIM
    const3 = lambda b, t: (0, 0, 0)
    return pl.pallas_call(
        functools.partial(_pool_retention_kernel, tile=tile, n_weights=len(round_weights)),
        grid=(batch, nt),
        in_specs=[pl.BlockSpec(memory_space=pltpu.SMEM),
                  pl.BlockSpec((tile, width), lambda b, t: (b * nt + t, 0)),
                  pl.BlockSpec((tile, RET_QK_DIM), lambda b, t: (t, 0)),
                  pl.BlockSpec((tile, RET_QK_DIM), lambda b, t: (t, 0)),
                  pl.BlockSpec(intra.shape, const3),
                  pl.BlockSpec(q_dec.shape, const3),
                  pl.BlockSpec(k_dec.shape, const3),
                  pl.BlockSpec(pool_w.shape, const3),
                  pl.BlockSpec((1, POOL_WIDTH), lambda b, t: (0, 0)),
                  pl.BlockSpec((1, vw), lambda b, t: (0, 0))] + w_in_specs,
        out_specs=[pl.BlockSpec((tile, POOL_WIDTH), lambda b, t: (b * nt + t, 0)),
                   pl.BlockSpec((tile, vw), lambda b, t: (b * nt + t, 0))] + w_out_specs,
        out_shape=[jax.ShapeDtypeStruct((m, POOL_WIDTH), BF16), jax.ShapeDtypeStruct((m, vw), BF16)] + w_out_shapes,
        scratch_shapes=[pltpu.VMEM((POOL_HALO, POOL_WIDTH), F32),
                        pltpu.VMEM((RET_HEADS, RET_QK_DIM, RET_V_DIM), F32)],
        compiler_params=_params("parallel", "arbitrary"),
        name="pool_retention",
    )(chunk_dec, z, cos, sin, intra, q_dec, k_dec, pool_w.astype(BF16),
      pool_scale.reshape(1, POOL_WIDTH), ret_gn_g.reshape(1, vw), *[w for w, _ in round_weights])


def _sgu_kernel(z_ref, lng_ref, ws_ref, bs_ref, o_ref, *, tile):
    zc = _gelu_tanh(z_ref[...].astype(F32))
    zu = zc[:, :SGU_WIDTH]
    v = _layer_norm(zc[:, SGU_WIDTH:], lng_ref[...]).astype(BF16)
    row = lax.broadcasted_iota(jnp.int32, (SGU_CHUNK, SGU_CHUNK), 0)
    col = lax.broadcasted_iota(jnp.int32, (SGU_CHUNK, SGU_CHUNK), 1)
    for g in range(SGU_GROUPS):
        wm = jnp.where(row >= col, ws_ref[g], 0.0).astype(BF16)
        bias = bs_ref[g]
        cols = slice(g * SGU_GROUP_DIM, (g + 1) * SGU_GROUP_DIM)
        for c in range(tile // SGU_CHUNK):
            rows = slice(c * SGU_CHUNK, (c + 1) * SGU_CHUNK)
            sv = jnp.dot(wm, v[rows, cols], preferred_element_type=F32) + bias
            o_ref[rows, cols] = (zu[rows, cols] * sv).astype(o_ref.dtype)


def spatial_gating(z, ln_g, w_s, b_s, *, tile=1024):
    m = z.shape[0]
    return pl.pallas_call(
        functools.partial(_sgu_kernel, tile=tile),
        grid=(m // tile,),
        in_specs=[pl.BlockSpec((tile, 2 * SGU_WIDTH), lambda i: (i, 0)),
                  pl.BlockSpec((1, SGU_WIDTH), lambda i: (0, 0)),
                  pl.BlockSpec(w_s.shape, lambda i: (0, 0, 0)),
                  pl.BlockSpec((SGU_GROUPS, SGU_CHUNK, 1), lambda i: (0, 0, 0))],
        out_specs=pl.BlockSpec((tile, SGU_WIDTH), lambda i: (i, 0)),
        out_shape=jax.ShapeDtypeStruct((m, SGU_WIDTH), BF16),
        compiler_params=_params("parallel"),
        name="spatial_gating",
    )(z, ln_g.reshape(1, SGU_WIDTH), w_s, b_s.reshape(SGU_GROUPS, SGU_CHUNK, 1))


def _t5_bucket(rel):
    n = np.maximum(rel, 0)
    max_exact = N_BUCKETS // 2
    large = max_exact + (np.log(np.maximum(n, 1).astype(np.float32) / np.float32(max_exact))
                         / np.float32(math.log(MAX_DISTANCE / max_exact))
                         * np.float32(N_BUCKETS - max_exact)).astype(np.int32)
    large = np.minimum(large, N_BUCKETS - 1)
    return np.where(n < max_exact, n, large).astype(np.int32)


def _bias_tiles_kernel(relb_ref, bucket_ref, o_ref):
    h = pl.program_id(0)
    bucket = bucket_ref[...]
    acc = jnp.zeros(bucket.shape, F32)
    for b in range(N_BUCKETS):
        acc = jnp.where(bucket == b, relb_ref[b, h], acc)
    far_bias = relb_ref[N_BUCKETS - 1, h]
    o_ref[0] = (acc - far_bias) * LOG2E


def rel_bias_tiles(rel_bias, *, tile):
    qk = np.arange(tile)[:, None] - np.arange(tile)[None, :]
    bucket = np.stack([_t5_bucket(qk), _t5_bucket(tile + qk)])
    return pl.pallas_call(
        _bias_tiles_kernel,
        grid=(DIFF_HEADS,),
        in_specs=[pl.BlockSpec(memory_space=pltpu.SMEM),
                  pl.BlockSpec((2, tile, tile), lambda h: (0, 0, 0))],
        out_specs=pl.BlockSpec((1, 2, tile, tile), lambda h: (h, 0, 0, 0)),
        out_shape=jax.ShapeDtypeStruct((DIFF_HEADS, 2, tile, tile), F32),
        compiler_params=_params("parallel"),
        name="rel_bias_tiles",
    )(rel_bias, bucket)


def _diff_attn_kernel(lq1_ref, lk1_ref, lq2_ref, lk2_ref, q_ref, k_ref, v_ref, bias_ref, subg_ref,
                      *rest, tile, seq, lam_init, n_weights):
    w_refs, o_ref, w_bf_refs = rest[:n_weights], rest[n_weights], rest[n_weights + 1:]
    _round_weight_slabs(w_refs, w_bf_refs)
    dh = DIFF_HEAD_DIM
    lam = (jnp.exp(jnp.sum(lq1_ref[...] * lk1_ref[...], axis=-1, keepdims=True))
           - jnp.exp(jnp.sum(lq2_ref[...] * lk2_ref[...], axis=-1, keepdims=True)) + lam_init)
    row = lax.broadcasted_iota(jnp.int32, (tile, tile), 0)
    col = lax.broadcasted_iota(jnp.int32, (tile, tile), 1)
    causal = row >= col

    def query_tile(qi):
        kv_len = (qi + 1) * tile
        rows = slice(qi * tile, kv_len)
        q = (q_ref[rows, :].astype(F32) * (dh ** -0.5 * LOG2E)).astype(BF16)
        streams = []
        for s in range(2):
            sc = lax.dot_general(q[:, s * dh:(s + 1) * dh], k_ref[:kv_len, s * dh:(s + 1) * dh],
                                 (((1,), (1,)), ((), ())), preferred_element_type=F32)
            parts = []
            if qi >= 2:
                parts.append(sc[:, :kv_len - 2 * tile])
            if qi >= 1:
                parts.append(sc[:, kv_len - 2 * tile:kv_len - tile] + bias_ref[0, 1])
            parts.append(jnp.where(causal, sc[:, kv_len - tile:] + bias_ref[0, 0], NEG_INF))
            sc = jnp.concatenate(parts, axis=1) if len(parts) > 1 else parts[0]
            p = jnp.exp2(sc - jnp.max(sc, axis=-1, keepdims=True))
            streams.append((p, jnp.sum(p, axis=-1, keepdims=True)))
        (p0, l0), (p1, l1) = streams
        attn = p0 * (1.0 / l0) - p1 * (lam / l1)
        d = jnp.dot(attn.astype(BF16), v_ref[:kv_len, :], preferred_element_type=F32)
        o_ref[rows, :] = (_rms(d, subg_ref[...]) * (1.0 - lam_init)).astype(o_ref.dtype)

    n_tiles = seq // tile
    always = pl.program_id(0) >= 0
    for first in range(n_tiles // 2):
        @pl.when(always)
        def _(first=first):
            query_tile(first)
            query_tile(n_tiles - 1 - first)


def diff_attention(z, bias_tiles, lq1, lk1, lq2, lk2, subln_g, round_weights, *, batch, seq, layer_idx,
                   tile):
    m = z.shape[0]
    assert tile >= MAX_DISTANCE and seq % (2 * tile) == 0
    w_in_specs, w_out_specs, w_out_shapes = _weight_rounding_specs(
        round_weights, batch * DIFF_HEADS, lambda b, h: b * DIFF_HEADS + h)
    w = 2 * DIFF_HEAD_DIM
    q_blk = 2 * SGU_WIDTH // w
    k_blk = q_blk + DIFF_HEADS
    v_blk = k_blk + DIFF_HEADS
    lam_init = 0.8 - 0.6 * math.exp(-0.3 * layer_idx)
    vec = lambda a: a.reshape(1, DIFF_HEAD_DIM)
    vec_spec = pl.BlockSpec((1, DIFF_HEAD_DIM), lambda b, h: (0, 0))
    return pl.pallas_call(
        functools.partial(_diff_attn_kernel, tile=tile, seq=seq, lam_init=lam_init, n_weights=len(round_weights)),
        grid=(batch, DIFF_HEADS),
        in_specs=[vec_spec, vec_spec, vec_spec, vec_spec,
                  pl.BlockSpec((seq, w), lambda b, h: (b, q_blk + h)),
                  pl.BlockSpec((seq, w), lambda b, h: (b, k_blk + h)),
                  pl.BlockSpec((seq, w), lambda b, h: (b, v_blk + h)),
                  pl.BlockSpec((1, 2, tile, tile), lambda b, h: (h, 0, 0, 0)),
                  pl.BlockSpec((1, DIFF_V_DIM), lambda b, h: (0, 0))] + w_in_specs,
        out_specs=[pl.BlockSpec((seq, DIFF_V_DIM), lambda b, h: (b, h))] + w_out_specs,
        out_shape=[jax.ShapeDtypeStruct((m, DIFF_HEADS * DIFF_V_DIM), BF16)] + w_out_shapes,
        compiler_params=_params("parallel", "parallel"),
        name="diff_attention",
    )(vec(lq1), vec(lk1), vec(lq2), vec(lk2), z, z, z, bias_tiles, subln_g.reshape(1, DIFF_V_DIM),
      *[w for w, _ in round_weights])


ATTN_TILE = 256
ROW_TILE_OUT = 512
ROW_TILE_DOWN = 256


def kernel(x, w_in_even, w_out_even, pool_w, pool_scale, ret_gn_g, w_in_odd, w_out_odd, sgu_ln_g, sgu_w, sgu_b,
           lam_q1, lam_k1, lam_q2, lam_k2, diff_subln_g, rel_bias, mix_norm_g, ffn_norm_g, w_up, conv_w, conv_b,
           w_down, final_norm_g):
    batch, seq, d = x.shape
    depth = mix_norm_g.shape[0]
    h = x.reshape(batch * seq, d)
    hn = rmsnorm(h, mix_norm_g[0])
    bias_tiles = rel_bias_tiles(rel_bias, tile=ATTN_TILE)
    out = None
    for i in range(depth):
        if i % 2 == 0:
            e = i // 2
            z = matmul(hn, w_in_even, e)
            a_out, b_out, w_out, w_dn = pool_retention(z, pool_w[e], pool_scale[e], ret_gn_g[e],
                                                        [(w_out_even, e), (w_down, i)], batch=batch, seq=seq)
            pieces = [a_out, b_out]
        else:
            o = i // 2
            z = matmul(hn, w_in_odd, o)
            c_out = spatial_gating(z, sgu_ln_g[o], sgu_w[o], sgu_b[o])
            d_out, w_out, w_dn = diff_attention(z, bias_tiles, lam_q1[o], lam_k1[o], lam_q2[o], lam_k2[o],
                                                diff_subln_g[o], [(w_out_odd, o), (w_down, i)],
                                                batch=batch, seq=seq, layer_idx=i, tile=ATTN_TILE)
            pieces = [c_out, d_out]
        h, hn = proj_res_norm(pieces, w_out, h, ffn_norm_g[i], tm=ROW_TILE_OUT)
        act = up_conv_gate(hn, w_up, i, conv_w[i], conv_b[i], seq=seq)
        if i + 1 < depth:
            h, hn = proj_res_norm([act], w_dn, h, mix_norm_g[i + 1], tm=ROW_TILE_DOWN)
        else:
            out = proj_res_norm([act], w_dn, h, final_norm_g, tm=ROW_TILE_DOWN,
                                norm_dtype=x.dtype, emit_h=False)
    return out.reshape(batch, seq, d)
```
